```python
import math
import jax, jax.numpy as jnp
from jax import lax
import numpy as np

D_MODEL = 4096
BATCH = 1
SEQ = 8192
DEPTH = 4
DEC_BATCH = 4
DEC_SEQ = 2048
PAST_LEN = 128

HEAD_DIM = 128
Q_BLOCK = 128
GRID_W = 64
ROPE_THETA = 10000.0
LN_EPS = 1e-5
RMS_EPS = 1e-6
N_MIXERS = 4
DEEPNORM_ALPHA = (2 * DEPTH) ** 0.25
DEEPNORM_BETA = (8 * DEPTH) ** -0.25
DA_HEADS = D_MODEL // (2 * HEAD_DIM)
MLA_HEADS = D_MODEL // HEAD_DIM
MLA_Q_LORA = D_MODEL // 4
MLA_KV_LORA = D_MODEL // 8
MLA_NOPE = 128
MLA_ROPE = 64
MLA_V = 128
NA_HEADS = D_MODEL // HEAD_DIM
NA_WIN_ROWS = 8
NA_WIN_COLS = 16
GQA_HEADS = D_MODEL // HEAD_DIM
GQA_KV_HEADS = 8
FFN_HIDDEN = 7 * D_MODEL // 2
N_EXPERTS = 8
TOP_K = 2
EXPERT_HIDDEN = 7 * D_MODEL // 2
MOE_BLOCK = 256

kernel_name = "hybrid_diff_mla_natten_gqa_deepnorm_encoder"


def lambda_init(layer_idx):
    return 0.8 - 0.6 * math.exp(-0.3 * layer_idx)


def layer_norm(x, g, b):
    xf = x.astype(jnp.float32)
    mu = jnp.mean(xf, axis=-1, keepdims=True)
    xc = xf - mu
    var = jnp.mean(xc * xc, axis=-1, keepdims=True)
    return (xc * lax.rsqrt(var + LN_EPS) * g + b).astype(x.dtype)


def rms_norm(x, g):
    xf = x.astype(jnp.float32)
    return (xf * lax.rsqrt(jnp.mean(xf * xf, axis=-1, keepdims=True) + RMS_EPS) * g).astype(x.dtype)


def rope_angles(pos, dim):
    inv = ROPE_THETA ** (-jnp.arange(0, dim, 2, dtype=jnp.float32) / dim)
    ang = pos[:, None] * inv[None, :]
    ang = jnp.concatenate([ang, ang], axis=-1)
    return jnp.cos(ang), jnp.sin(ang)


def apply_rope(x, cos, sin):
    half = x.shape[-1] // 2
    rot = jnp.concatenate([-x[..., half:], x[..., :half]], axis=-1)
    return (x * cos[None, :, None, :] + rot * sin[None, :, None, :]).astype(x.dtype)


def blocked_attention(q, k, v, scale):
    B, S, G, R, dq = q.shape
    nb = S // Q_BLOCK
    qb = q.reshape(B, nb, Q_BLOCK, G, R, dq).swapaxes(0, 1)

    def one_block(q_blk):
        s = jnp.einsum('bqgrd,bkgd->bgrqk', q_blk, k, preferred_element_type=jnp.float32) * scale
        p = jax.nn.softmax(s, axis=-1).astype(v.dtype)
        return jnp.einsum('bgrqk,bkgd->bqgrd', p, v)

    out = lax.map(one_block, qb)
    return out.swapaxes(0, 1).reshape(B, S, G, R, v.shape[-1])


def diff_attention(x, w_qkv, lq1, lk1, lq2, lk2, subln_g, w_o, lam_init):
    B, S, _ = x.shape
    q, k, v = jnp.split(x @ w_qkv, 3, axis=-1)
    cos, sin = rope_angles(jnp.arange(S, dtype=jnp.float32), HEAD_DIM)
    q = apply_rope(q.reshape(B, S, 2 * DA_HEADS, HEAD_DIM), cos, sin).reshape(B, S, DA_HEADS, 2, HEAD_DIM)
    k = apply_rope(k.reshape(B, S, 2 * DA_HEADS, HEAD_DIM), cos, sin).reshape(B, S, DA_HEADS, 2, HEAD_DIM)
    v = v.reshape(B, S, DA_HEADS, 2 * HEAD_DIM)
    f32 = jnp.float32
    lam = (jnp.exp(jnp.sum(lq1.astype(f32) * lk1.astype(f32)))
           - jnp.exp(jnp.sum(lq2.astype(f32) * lk2.astype(f32))) + lam_init)
    scale = HEAD_DIM ** -0.5
    nb = S // Q_BLOCK
    qb = q.reshape(B, nb, Q_BLOCK, DA_HEADS, 2, HEAD_DIM).swapaxes(0, 1)

    def one_block(q_blk):
        s = jnp.einsum('bqhcd,bkhcd->bhcqk', q_blk, k, preferred_element_type=f32) * scale
        p = jax.nn.softmax(s, axis=-1)
        w = p[:, :, 0] - lam * p[:, :, 1]
        return jnp.einsum('bhqk,bkhe->bqhe', w.astype(v.dtype), v)

    o = lax.map(one_block, qb).swapaxes(0, 1).reshape(B, S, DA_HEADS, 2 * HEAD_DIM)
    o = rms_norm(o, subln_g) * (1.0 - lam_init)
    return o.reshape(B, S, D_MODEL) @ w_o


def mla_attention(x, w_dqkv, q_norm_g, w_uq, kv_norm_g, w_ukv, w_o):
    B, S, _ = x.shape
    down = x @ w_dqkv
    c_q, c_kv, k_rope = jnp.split(down, [MLA_Q_LORA, MLA_Q_LORA + MLA_KV_LORA], axis=-1)
    q = (rms_norm(c_q, q_norm_g) @ w_uq).reshape(B, S, MLA_HEADS, MLA_NOPE + MLA_ROPE)
    kv = (rms_norm(c_kv, kv_norm_g) @ w_ukv).reshape(B, S, MLA_HEADS, MLA_NOPE + MLA_V)
    q_nope, q_rope = q[..., :MLA_NOPE], q[..., MLA_NOPE:]
    k_nope, v = kv[..., :MLA_NOPE], kv[..., MLA_NOPE:]
    cos, sin = rope_angles(jnp.arange(S, dtype=jnp.float32), MLA_ROPE)
    q_rope = apply_rope(q_rope, cos, sin)
    k_rope = apply_rope(k_rope[:, :, None, :], cos, sin)
    q = jnp.concatenate([q_nope, q_rope], axis=-1)[:, :, :, None, :]
    k = jnp.concatenate([k_nope, jnp.broadcast_to(k_rope, (B, S, MLA_HEADS, MLA_ROPE))], axis=-1)
    o = blocked_attention(q, k, v, (MLA_NOPE + MLA_ROPE) ** -0.5)
    return o.reshape(B, S, MLA_HEADS * MLA_V) @ w_o


def neighbourhood_attention(x, w_qkv, rpb, w_o):
    B, S, _ = x.shape
    R = S // GRID_W
    wr = min(NA_WIN_ROWS, R)
    qkv = (x @ w_qkv).reshape(B, R, GRID_W, 3, NA_HEADS, HEAD_DIM)
    q, k, v = qkv[:, :, :, 0], qkv[:, :, :, 1], qkv[:, :, :, 2]
    c = np.arange(GRID_W)
    cs = np.clip(c - NA_WIN_COLS // 2, 0, GRID_W - NA_WIN_COLS)
    col_in = (c[None, :] >= cs[:, None]) & (c[None, :] < cs[:, None] + NA_WIN_COLS)
    mask = np.tile(col_in[:, None, :], (1, wr, 1)).reshape(GRID_W, wr * GRID_W)
    dc_idx = np.clip(c[None, :] - c[:, None], -(NA_WIN_COLS - 1), NA_WIN_COLS - 1) + NA_WIN_COLS - 1
    scale = HEAD_DIM ** -0.5

    def one_row(args):
        r, q_row = args
        rs = jnp.clip(r - wr // 2, 0, R - wr)
        k_blk = lax.dynamic_slice_in_dim(k, rs, wr, axis=1).reshape(B, wr * GRID_W, NA_HEADS, HEAD_DIM)
        v_blk = lax.dynamic_slice_in_dim(v, rs, wr, axis=1).reshape(B, wr * GRID_W, NA_HEADS, HEAD_DIM)
        dr_idx = rs + jnp.arange(wr, dtype=jnp.int32) - r + NA_WIN_ROWS - 1
        bias = rpb[:, dr_idx[None, :, None], dc_idx[:, None, :]]
        bias = bias.reshape(NA_HEADS, GRID_W, wr * GRID_W).astype(jnp.float32)
        s = jnp.einsum('bqhd,bkhd->bhqk', q_row, k_blk, preferred_element_type=jnp.float32) * scale + bias[None]
        s = jnp.where(mask[None, None], s, -jnp.inf)
        p = jax.nn.softmax(s, axis=-1).astype(v.dtype)
        return jnp.einsum('bhqk,bkhd->bqhd', p, v_blk)

    out = lax.map(one_row, (jnp.arange(R, dtype=jnp.int32), q.swapaxes(0, 1)))
    return out.swapaxes(0, 1).reshape(B, S, D_MODEL) @ w_o


def gqa_axial_attention(x, w_qkv, q_norm_g, k_norm_g, w_o):
    B, S, _ = x.shape
    kv_w = GQA_KV_HEADS * HEAD_DIM
    q, k, v = jnp.split(x @ w_qkv, [D_MODEL, D_MODEL + kv_w], axis=-1)
    q = rms_norm(q.reshape(B, S, GQA_HEADS, HEAD_DIM), q_norm_g)
    k = rms_norm(k.reshape(B, S, GQA_KV_HEADS, HEAD_DIM), k_norm_g)
    v = v.reshape(B, S, GQA_KV_HEADS, HEAD_DIM)
    half = HEAD_DIM // 2
    t = jnp.arange(S, dtype=jnp.int32)
    cr, sr = rope_angles((t // GRID_W).astype(jnp.float32), half)
    cc, sc = rope_angles((t % GRID_W).astype(jnp.float32), half)

    def axial(z):
        return jnp.concatenate([apply_rope(z[..., :half], cr, sr), apply_rope(z[..., half:], cc, sc)], axis=-1)

    q = axial(q).reshape(B, S, GQA_KV_HEADS, GQA_HEADS // GQA_KV_HEADS, HEAD_DIM)
    k = axial(k)
    o = blocked_attention(q, k, v, HEAD_DIM ** -0.5)
    return o.reshape(B, S, D_MODEL) @ w_o


def swiglu(x, w_gate_up, w_down):
    g, u = jnp.split(x @ w_gate_up, 2, axis=-1)
    return (jax.nn.silu(g) * u) @ w_down


def moe_swiglu(x, w_router, w_gate_up, w_down):
    B, S, D = x.shape
    n_tok = B * S
    n_asg = n_tok * TOP_K
    xt = x.reshape(n_tok, D)
    logits = jnp.dot(xt, w_router, preferred_element_type=jnp.float32)
    top_val, top_idx = lax.top_k(logits, TOP_K)
    gates = jax.nn.softmax(top_val, axis=-1)
    flat_e = top_idx.reshape(-1).astype(jnp.int32)
    flat_tok = jnp.repeat(jnp.arange(n_tok, dtype=jnp.int32), TOP_K)
    flat_g = gates.reshape(-1)
    order = jnp.argsort(flat_e)
    e_sorted = flat_e[order]
    counts = jnp.bincount(flat_e, length=N_EXPERTS).astype(jnp.int32)
    padded = (counts + MOE_BLOCK - 1) // MOE_BLOCK * MOE_BLOCK
    padded_end = jnp.cumsum(padded)
    start = jnp.cumsum(counts) - counts
    dest = (padded_end - padded)[e_sorted] + jnp.arange(n_asg, dtype=jnp.int32) - start[e_sorted]
    n_slots = -(-n_asg // MOE_BLOCK) * MOE_BLOCK + N_EXPERTS * MOE_BLOCK
    n_groups = n_slots // MOE_BLOCK
    slot_tok = jnp.zeros((n_slots,), jnp.int32).at[dest].set(flat_tok[order])
    slot_gate = jnp.zeros((n_slots,), jnp.float32).at[dest].set(flat_g[order])
    group_start = jnp.arange(n_groups, dtype=jnp.int32) * MOE_BLOCK
    group_expert = jnp.minimum(jnp.sum(padded_end[None, :] <= group_start[:, None], axis=1), N_EXPERTS - 1)
    xs = xt[slot_tok].reshape(n_groups, MOE_BLOCK, D)

    def expert_group(args):
        e, xg = args
        g, u = jnp.split(xg @ w_gate_up[e], 2, axis=-1)
        return (jax.nn.silu(g) * u) @ w_down[e]

    ys = lax.map(expert_group, (group_expert, xs)).reshape(n_slots, D)
    out = jnp.zeros((n_tok, D), jnp.float32).at[slot_tok].add(ys * slot_gate[:, None])
    return out.astype(x.dtype).reshape(B, S, D)


def setup_inputs(seed: int = 0) -> dict:
    key = jax.random.key(seed)
    keys = iter(jax.random.split(key, 64))
    f32 = jnp.float32
    D, F, Fe, d = D_MODEL, FFN_HIDDEN, EXPERT_HIDDEN, HEAD_DIM

    def w(shape, fan_in, gain=1.0):
        return jax.random.normal(next(keys), shape, f32) * (gain * fan_in ** -0.5)

    def gain_vec(n):
        return 1.0 + 0.01 * jax.random.normal(next(keys), (n,), f32)

    def bias_vec(n):
        return 0.01 * jax.random.normal(next(keys), (n,), f32)

    def small(shape, s):
        return s * jax.random.normal(next(keys), shape, f32)

    p = {}
    p["x_prompt"] = jax.random.normal(next(keys), (BATCH, SEQ, D), f32)
    p["x_sample"] = jax.random.normal(next(keys), (DEC_BATCH, DEC_SEQ, D), f32)
    p["l0_w_qkv"] = w((D, 3 * D), D)
    p["l0_lambda_q1"] = small((d,), 0.1)
    p["l0_lambda_k1"] = small((d,), 0.1)
    p["l0_lambda_q2"] = small((d,), 0.1)
    p["l0_lambda_k2"] = small((d,), 0.1)
    p["l0_subln_g"] = gain_vec(2 * d)
    p["l0_w_o"] = w((D, D), D, DEEPNORM_BETA)
    p["l0_ln1_g"] = gain_vec(D)
    p["l0_ln1_b"] = bias_vec(D)
    p["l0_ffn_w_gate_up"] = w((D, 2 * F), D)
    p["l0_ffn_w_down"] = w((F, D), F, DEEPNORM_BETA)
    p["l0_ln2_g"] = gain_vec(D)
    p["l0_ln2_b"] = bias_vec(D)
    p["l1_w_dqkv"] = w((D, MLA_Q_LORA + MLA_KV_LORA + MLA_ROPE), D)
    p["l1_q_norm_g"] = gain_vec(MLA_Q_LORA)
    p["l1_w_uq"] = w((MLA_Q_LORA, MLA_HEADS * (MLA_NOPE + MLA_ROPE)), MLA_Q_LORA)
    p["l1_kv_norm_g"] = gain_vec(MLA_KV_LORA)
    p["l1_w_ukv"] = w((MLA_KV_LORA, MLA_HEADS * (MLA_NOPE + MLA_V)), MLA_KV_LORA)
    p["l1_w_o"] = w((MLA_HEADS * MLA_V, D), MLA_HEADS * MLA_V, DEEPNORM_BETA)
    p["l1_ln1_g"] = gain_vec(D)
    p["l1_ln1_b"] = bias_vec(D)
    p["l1_router"] = w((D, N_EXPERTS), D)
    p["l1_exp_w_gate_up"] = w((N_EXPERTS, D, 2 * Fe), D)
    p["l1_exp_w_down"] = w((N_EXPERTS, Fe, D), Fe, DEEPNORM_BETA)
    p["l1_ln2_g"] = gain_vec(D)
    p["l1_ln2_b"] = bias_vec(D)
    p["l2_w_qkv"] = w((D, 3 * D), D)
    p["l2_rpb"] = small((NA_HEADS, 2 * NA_WIN_ROWS - 1, 2 * NA_WIN_COLS - 1), 0.02)
    p["l2_w_o"] = w((D, D), D, DEEPNORM_BETA)
    p["l2_ln1_g"] = gain_vec(D)
    p["l2_ln1_b"] = bias_vec(D)
    p["l2_ffn_w_gate_up"] = w((D, 2 * F), D)
    p["l2_ffn_w_down"] = w((F, D), F, DEEPNORM_BETA)
    p["l2_ln2_g"] = gain_vec(D)
    p["l2_ln2_b"] = bias_vec(D)
    p["l3_w_qkv"] = w((D, D + 2 * GQA_KV_HEADS * d), D)
    p["l3_q_norm_g"] = gain_vec(d)
    p["l3_k_norm_g"] = gain_vec(d)
    p["l3_w_o"] = w((D, D), D, DEEPNORM_BETA)
    p["l3_ln1_g"] = gain_vec(D)
    p["l3_ln1_b"] = bias_vec(D)
    p["l3_router"] = w((D, N_EXPERTS), D)
    p["l3_exp_w_gate_up"] = w((N_EXPERTS, D, 2 * Fe), D)
    p["l3_exp_w_down"] = w((N_EXPERTS, Fe, D), Fe, DEEPNORM_BETA)
    p["l3_ln2_g"] = gain_vec(D)
    p["l3_ln2_b"] = bias_vec(D)
    return p


def reference(x_prompt, x_sample,
              l0_w_qkv, l0_lambda_q1, l0_lambda_k1, l0_lambda_q2, l0_lambda_k2, l0_subln_g, l0_w_o,
              l0_ln1_g, l0_ln1_b, l0_ffn_w_gate_up, l0_ffn_w_down, l0_ln2_g, l0_ln2_b,
              l1_w_dqkv, l1_q_norm_g, l1_w_uq, l1_kv_norm_g, l1_w_ukv, l1_w_o,
              l1_ln1_g, l1_ln1_b, l1_router, l1_exp_w_gate_up, l1_exp_w_down, l1_ln2_g, l1_ln2_b,
              l2_w_qkv, l2_rpb, l2_w_o, l2_ln1_g, l2_ln1_b, l2_ffn_w_gate_up, l2_ffn_w_down, l2_ln2_g, l2_ln2_b,
              l3_w_qkv, l3_q_norm_g, l3_k_norm_g, l3_w_o, l3_ln1_g, l3_ln1_b,
              l3_router, l3_exp_w_gate_up, l3_exp_w_down, l3_ln2_g, l3_ln2_b):
    mixers = (
        lambda h: diff_attention(h, l0_w_qkv, l0_lambda_q1, l0_lambda_k1, l0_lambda_q2, l0_lambda_k2,
                                 l0_subln_g, l0_w_o, lambda_init(0)),
        lambda h: mla_attention(h, l1_w_dqkv, l1_q_norm_g, l1_w_uq, l1_kv_norm_g, l1_w_ukv, l1_w_o),
        lambda h: neighbourhood_attention(h, l2_w_qkv, l2_rpb, l2_w_o),
        lambda h: gqa_axial_attention(h, l3_w_qkv, l3_q_norm_g, l3_k_norm_g, l3_w_o),
    )
    channel = (
        lambda h: swiglu(h, l0_ffn_w_gate_up, l0_ffn_w_down),
        lambda h: moe_swiglu(h, l1_router, l1_exp_w_gate_up, l1_exp_w_down),
        lambda h: swiglu(h, l2_ffn_w_gate_up, l2_ffn_w_down),
        lambda h: moe_swiglu(h, l3_router, l3_exp_w_gate_up, l3_exp_w_down),
    )
    norms = (
        (l0_ln1_g, l0_ln1_b, l0_ln2_g, l0_ln2_b),
        (l1_ln1_g, l1_ln1_b, l1_ln2_g, l1_ln2_b),
        (l2_ln1_g, l2_ln1_b, l2_ln2_g, l2_ln2_b),
        (l3_ln1_g, l3_ln1_b, l3_ln2_g, l3_ln2_b),
    )

    def trunk(x):
        for i in range(DEPTH):
            g1, b1, g2, b2 = norms[i]
            x = layer_norm(DEEPNORM_ALPHA * x + mixers[i % N_MIXERS](x), g1, b1)
            x = layer_norm(DEEPNORM_ALPHA * x + channel[i](x), g2, b2)
        return x

    y_prompt = trunk(x_prompt)
    y_sample = trunk(x_sample)
    return (y_prompt, y_sample)
```

```python
import functools
import math

import numpy as np
import jax
import jax.numpy as jnp
from jax import lax
from jax.experimental import pallas as pl
from jax.experimental.pallas import tpu as pltpu

HEAD_DIM = 128
GRID_W = 64
ROPE_THETA = 10000.0
LN_EPS = 1e-5
RMS_EPS = 1e-6
DEPTH = 4
DEEPNORM_ALPHA = (2 * DEPTH) ** 0.25
MLA_NOPE = 128
MLA_ROPE = 64
MLA_V = 128
NA_WIN_ROWS = 8
NA_WIN_COLS = 16
GQA_KV_HEADS = 8
N_EXPERTS = 8
TOP_K = 2
MOE_GROUP = 256
MASK_VALUE = -1e30

V7X_VMEM_LIMIT_BYTES = 56 * 1024 * 1024
LANES = 128

F32 = jnp.float32
BF16 = jnp.bfloat16


def _cparams(*sem):
    return pltpu.CompilerParams(dimension_semantics=sem, vmem_limit_bytes=V7X_VMEM_LIMIT_BYTES)


def _pick(n, pref):
    if n <= pref:
        return n
    t = pref
    while n % t:
        t //= 2
    return t


def _rot_half(xs, w, first):
    tw = xs.shape[1]
    if w == tw:
        return pltpu.roll(xs, w // 2, 1)
    return jnp.where(first, pltpu.roll(xs, tw - w // 2, 1), pltpu.roll(xs, w // 2, 1))


def _mm_kernel(*refs, expert, swiglu, rms_w, rope_w, scale):
    refs = list(refs)
    if expert:
        refs.pop(0)
    a_ref = refs.pop(0)
    b_ref = refs.pop(0)
    bu_ref = refs.pop(0) if swiglu else None
    gain_ref = refs.pop(0) if rms_w else None
    cos_ref = refs.pop(0) if rope_w else None
    sin_ref = refs.pop(0) if rope_w else None
    o_ref = refs.pop(0)

    a = a_ref[...]
    acc = jnp.dot(a, b_ref[...], preferred_element_type=F32)
    if swiglu:
        up = jnp.dot(a, bu_ref[...], preferred_element_type=F32)
        acc = acc * jax.nn.sigmoid(acc) * up
    tm, tn = acc.shape
    if rms_w:
        parts = []
        for c in range(tn // rms_w):
            xs = acc[:, c * rms_w:(c + 1) * rms_w]
            ms = jnp.mean(xs * xs, axis=-1, keepdims=True)
            parts.append(xs * lax.rsqrt(ms + RMS_EPS) * gain_ref[:, c * rms_w:(c + 1) * rms_w])
        acc = parts[0] if len(parts) == 1 else jnp.concatenate(parts, axis=1)
    if rope_w:
        tw = cos_ref.shape[1]
        cos = cos_ref[...]
        sin = sin_ref[...]
        lane = lax.broadcasted_iota(jnp.int32, (tm, tw), 1)
        first = (lane % rope_w) < (rope_w // 2)
        for c in range(tn // tw):
            xs = acc[:, c * tw:(c + 1) * tw]
            xs = xs * cos + _rot_half(xs, rope_w, first) * sin
            if scale != 1.0:
                xs = xs * scale
            o_ref[:, c * tw:(c + 1) * tw] = xs.astype(o_ref.dtype)
    else:
        if scale != 1.0:
            acc = acc * scale
        o_ref[...] = acc.astype(o_ref.dtype)


def _mm(a, b, *, n_out, col0=0, tm=1024, tn=512, out_dtype=BF16, swiglu=False,
        group_expert=None, rms_gain=None, rms_w=None, rope=None, rope_w=None, scale=1.0):
    m, k = a.shape
    tm = _pick(m, tm)
    tn = _pick(n_out, tn)
    assert m % tm == 0 and n_out % tn == 0 and col0 % tn == 0
    expert = group_expert is not None
    cb = col0 // tn
    ub = (b.shape[-1] // 2) // tn
    grid = (n_out // tn, m // tm)

    if expert:
        a_map = lambda j, i, ge: (i, 0)
        b_map = lambda j, i, ge: (ge[i], 0, cb + j)
        u_map = lambda j, i, ge: (ge[i], 0, cb + ub + j)
        c_map = lambda j, i, ge: (0, j)
        t_map = lambda j, i, ge: (i, 0)
        o_map = lambda j, i, ge: (i, j)
        b_block = (None, k, tn)
    else:
        a_map = lambda j, i: (i, 0)
        b_map = lambda j, i: (0, cb + j)
        u_map = lambda j, i: (0, cb + ub + j)
        c_map = lambda j, i: (0, j)
        t_map = lambda j, i: (i, 0)
        o_map = lambda j, i: (i, j)
        b_block = (k, tn)

    in_specs = [pl.BlockSpec((tm, k), a_map), pl.BlockSpec(b_block, b_map)]
    args = [a, b]
    if swiglu:
        assert b.shape[-1] % (2 * tn) == 0
        in_specs.append(pl.BlockSpec(b_block, u_map))
        args.append(b)
    if rms_w:
        assert tn % rms_w == 0
        in_specs.append(pl.BlockSpec((1, tn), c_map))
        args.append(rms_gain)
    if rope_w:
        cos, sin = rope
        tw = cos.shape[1]
        assert tn % tw == 0 and cos.shape[0] == m
        in_specs += [pl.BlockSpec((tm, tw), t_map), pl.BlockSpec((tm, tw), t_map)]
        args += [cos, sin]

    kern = functools.partial(_mm_kernel, expert=expert, swiglu=swiglu, rms_w=rms_w,
                             rope_w=rope_w, scale=float(scale))
    out_shape = jax.ShapeDtypeStruct((m, n_out), out_dtype)
    out_spec = pl.BlockSpec((tm, tn), o_map)
    if expert:
        gs = pltpu.PrefetchScalarGridSpec(num_scalar_prefetch=1, grid=grid,
                                          in_specs=in_specs, out_specs=out_spec)
        return pl.pallas_call(kern, out_shape=out_shape, grid_spec=gs,
                              compiler_params=_cparams("parallel", "parallel"),
                              name="expert_mm")(group_expert, *args)
    return pl.pallas_call(kern, out_shape=out_shape, grid=grid, in_specs=in_specs,
                          out_specs=out_spec, compiler_params=_cparams("parallel", "parallel"),
                          name="mm")(*args)


def _layer_norm_rows(z, g, b):
    mu = jnp.mean(z, axis=-1, keepdims=True)
    zc = z - mu
    var = jnp.mean(zc * zc, axis=-1, keepdims=True)
    return zc * lax.rsqrt(var + LN_EPS) * g + b


def _ln_kernel(x_ref, y_ref, g_ref, b_ref, o32_ref, o16_ref):
    z = DEEPNORM_ALPHA * x_ref[...] + y_ref[...].astype(F32)
    out = _layer_norm_rows(z, g_ref[...], b_ref[...])
    o32_ref[...] = out
    o16_ref[...] = out.astype(BF16)


def _residual_ln(x, y, g, b, *, tm=256):
    m, d = x.shape
    tm = _pick(m, tm)
    row = pl.BlockSpec((tm, d), lambda i: (i, 0))
    vec = pl.BlockSpec((1, d), lambda i: (0, 0))
    return pl.pallas_call(
        _ln_kernel,
        out_shape=(jax.ShapeDtypeStruct((m, d), F32), jax.ShapeDtypeStruct((m, d), BF16)),
        grid=(m // tm,), in_specs=[row, row, vec, vec], out_specs=(row, row),
        compiler_params=_cparams("parallel"), name="residual_ln",
    )(x, y, g.reshape(1, d), b.reshape(1, d))


def _flash_kernel(*refs, mode, n_rep, nk, lam_init):
    refs = list(refs)
    q_ref = refs.pop(0)
    k_ref = refs.pop(0)
    kr_ref = refs.pop(0) if mode == "mla" else None
    v_ref = refs.pop(0)
    if mode == "diff":
        lq1_ref, lk1_ref, lq2_ref, lk2_ref, sg_ref = [refs.pop(0) for _ in range(5)]
    o_ref = refs.pop(0)
    m_s, l_s, acc_s = refs

    kk = pl.program_id(3)

    @pl.when(kk == 0)
    def _():
        m_s[...] = jnp.full_like(m_s, -jnp.inf)
        l_s[...] = jnp.zeros_like(l_s)
        acc_s[...] = jnp.zeros_like(acc_s)

    q = q_ref[...]
    k = k_ref[...]
    if mode == "mla":
        k = jnp.concatenate([k, kr_ref[...]], axis=1)
    v = v_ref[...]
    dq = q.shape[1] // n_rep
    for r in range(n_rep):
        q_r = q[:, r * dq:(r + 1) * dq]
        k_r = k[:, r * dq:(r + 1) * dq] if mode == "diff" else k
        s = lax.dot_general(q_r, k_r, (((1,), (1,)), ((), ())), preferred_element_type=F32)
        m_prev = m_s[r]
        m_new = jnp.maximum(m_prev, jnp.max(s, axis=-1, keepdims=True))
        alpha = jnp.exp(m_prev - m_new)
        p = jnp.exp(s - m_new)
        l_s[r] = alpha * l_s[r] + jnp.sum(p, axis=-1, keepdims=True)
        acc_s[r] = alpha * acc_s[r] + jnp.dot(p.astype(BF16), v, preferred_element_type=F32)
        m_s[r] = m_new

    @pl.when(kk == nk - 1)
    def _():
        if mode == "diff":
            lam = (jnp.exp(jnp.sum(lq1_ref[...] * lk1_ref[...], axis=-1, keepdims=True))
                   - jnp.exp(jnp.sum(lq2_ref[...] * lk2_ref[...], axis=-1, keepdims=True)) + lam_init)
            o = acc_s[0] / l_s[0] - lam * (acc_s[1] / l_s[1])
            ms = jnp.mean(o * o, axis=-1, keepdims=True)
            o = o * lax.rsqrt(ms + RMS_EPS) * sg_ref[...] * (1.0 - lam_init)
            o_ref[...] = o.astype(o_ref.dtype)
        else:
            dv = acc_s.shape[2]
            for r in range(n_rep):
                o_ref[:, r * dv:(r + 1) * dv] = (acc_s[r] / l_s[r]).astype(o_ref.dtype)


def _flash(mode, q, k, v, *, row0, batch, seq, n_groups, kr=None, diff_params=None,
           lam_init=0.0, k_col=None, v_col=None, tq=512, tk=1024):
    tq = _pick(seq, tq)
    tk = _pick(seq, tk)
    nq, nk = seq // tq, seq // tk
    rq, rk = row0 // tq, row0 // tk
    assert row0 % tq == 0 and row0 % tk == 0
    if mode == "diff":
        n_rep, qw, kw, vw, ow = 2, 2 * HEAD_DIM, 2 * HEAD_DIM, 2 * HEAD_DIM, 2 * HEAD_DIM
    elif mode == "gqa":
        n_rep = q.shape[1] // (n_groups * HEAD_DIM)
        qw, kw, vw, ow = n_rep * HEAD_DIM, HEAD_DIM, HEAD_DIM, n_rep * HEAD_DIM
    else:
        n_rep, qw, kw, vw, ow = 1, 2 * HEAD_DIM, HEAD_DIM, HEAD_DIM, HEAD_DIM
    k_col = k_col or (lambda h: h)
    v_col = v_col or (lambda h: h)

    in_specs = [
        pl.BlockSpec((tq, qw), lambda b, h, i, kk: (rq + b * nq + i, h)),
        pl.BlockSpec((tk, kw), lambda b, h, i, kk: (rk + b * nk + kk, k_col(h))),
    ]
    args = [q, k]
    if mode == "mla":
        in_specs.append(pl.BlockSpec((tk, HEAD_DIM), lambda b, h, i, kk: (rk + b * nk + kk, 0)))
        args.append(kr)
    in_specs.append(pl.BlockSpec((tk, vw), lambda b, h, i, kk: (rk + b * nk + kk, v_col(h))))
    args.append(v)
    if mode == "diff":
        for p in diff_params:
            in_specs.append(pl.BlockSpec((1, p.shape[1]), lambda b, h, i, kk: (0, 0)))
            args.append(p)
    dv = vw
    kern = functools.partial(_flash_kernel, mode=mode, n_rep=n_rep, nk=nk, lam_init=lam_init)
    return pl.pallas_call(
        kern,
        out_shape=jax.ShapeDtypeStruct((batch * seq, n_groups * ow), BF16),
        grid=(batch, n_groups, nq, nk),
        in_specs=in_specs,
        out_specs=pl.BlockSpec((tq, ow), lambda b, h, i, kk: (b * nq + i, h)),
        scratch_shapes=[pltpu.VMEM((n_rep, tq, 1), F32), pltpu.VMEM((n_rep, tq, 1), F32),
                        pltpu.VMEM((n_rep, tq, dv), F32)],
        compiler_params=_cparams("parallel", "parallel", "parallel", "arbitrary"),
        name="flash_" + mode,
    )(*args)


def _na_kernel(q_ref, k_ref, v_ref, bias_ref, o_ref, *, n_rows, heads, scale):
    r = pl.program_id(2)
    rs = jnp.clip(r - NA_WIN_ROWS // 2, 0, n_rows - NA_WIN_ROWS)
    start = pl.multiple_of(rs * GRID_W, GRID_W)
    kb = k_ref[pl.ds(start, NA_WIN_ROWS * GRID_W), :]
    vb = v_ref[pl.ds(start, NA_WIN_ROWS * GRID_W), :]
    q = q_ref[...]
    for h in range(heads):
        sl = slice(h * HEAD_DIM, (h + 1) * HEAD_DIM)
        s = lax.dot_general(q[:, sl], kb[:, sl], (((1,), (1,)), ((), ())), preferred_element_type=F32)
        s = s * scale + bias_ref[h]
        m = jnp.max(s, axis=-1, keepdims=True)
        p = jnp.exp(s - m)
        l = jnp.sum(p, axis=-1, keepdims=True)
        o = jnp.dot(p.astype(BF16), vb[:, sl], preferred_element_type=F32) / l
        o_ref[:, sl] = o.astype(o_ref.dtype)


def _na_bias_table(rpb):
    c = np.arange(GRID_W)
    cs = np.clip(c - NA_WIN_COLS // 2, 0, GRID_W - NA_WIN_COLS)
    col_in = (c[None, :] >= cs[:, None]) & (c[None, :] < cs[:, None] + NA_WIN_COLS)
    dc_idx = np.clip(c[None, :] - c[:, None], -(NA_WIN_COLS - 1), NA_WIN_COLS - 1) + NA_WIN_COLS - 1
    case = np.arange(NA_WIN_ROWS)
    dr_idx = np.arange(NA_WIN_ROWS)[None, :] - case[:, None] + NA_WIN_ROWS - 1
    t = rpb.astype(F32)[:, dr_idx[:, None, :, None], dc_idx[None, :, None, :]]
    t = jnp.where(col_in[None, None, :, None, :], t, MASK_VALUE)
    return t.reshape(rpb.shape[0], NA_WIN_ROWS, GRID_W, NA_WIN_ROWS * GRID_W)


def _neighbourhood(qkv, bias, *, row0, batch, seq, n_heads, heads_per_step=4):
    n_rows = seq // GRID_W
    assert n_rows >= NA_WIN_ROWS
    hb = heads_per_step
    nhb = n_heads // hb
    r0 = row0 // GRID_W
    b0 = row0 // seq
    assert row0 % seq == 0

    def bias_map(b, h, r):
        rs = jnp.clip(r - NA_WIN_ROWS // 2, 0, n_rows - NA_WIN_ROWS)
        return (h, r - rs, 0, 0)

    kern = functools.partial(_na_kernel, n_rows=n_rows, heads=hb, scale=HEAD_DIM ** -0.5)
    return pl.pallas_call(
        kern,
        out_shape=jax.ShapeDtypeStruct((batch * seq, n_heads * HEAD_DIM), BF16),
        grid=(batch, nhb, n_rows),
        in_specs=[
            pl.BlockSpec((GRID_W, hb * HEAD_DIM), lambda b, h, r: (r0 + b * n_rows + r, h)),
            pl.BlockSpec((seq, hb * HEAD_DIM), lambda b, h, r: (b0 + b, nhb + h)),
            pl.BlockSpec((seq, hb * HEAD_DIM), lambda b, h, r: (b0 + b, 2 * nhb + h)),
            pl.BlockSpec((hb, None, GRID_W, NA_WIN_ROWS * GRID_W), bias_map),
        ],
        out_specs=pl.BlockSpec((GRID_W, hb * HEAD_DIM), lambda b, h, r: (b * n_rows + r, h)),
        compiler_params=_cparams("parallel", "parallel", "arbitrary"),
        name="neighbourhood_attention",
    )(qkv, qkv, qkv, bias)


def _router_kernel(x_ref, w_ref, o_ref):
    logits = jnp.dot(x_ref[...], w_ref[...], preferred_element_type=F32,
                     precision=lax.Precision.HIGHEST)
    lane = lax.broadcasted_iota(jnp.int32, logits.shape, 1)
    logits = jnp.where(lane < N_EXPERTS, logits, -jnp.inf)
    m1 = jnp.max(logits, axis=-1, keepdims=True)
    i1 = jnp.min(jnp.where(logits == m1, lane, LANES), axis=-1, keepdims=True)
    rest = jnp.where(lane == i1, -jnp.inf, logits)
    m2 = jnp.max(rest, axis=-1, keepdims=True)
    i2 = jnp.min(jnp.where(rest == m2, lane, LANES), axis=-1, keepdims=True)
    e2 = jnp.exp(m2 - m1)
    g1 = 1.0 / (1.0 + e2)
    g2 = e2 / (1.0 + e2)
    out = jnp.where(lane == 0, i1.astype(F32), 0.0)
    out = jnp.where(lane == 1, i2.astype(F32), out)
    out = jnp.where(lane == 2, g1, out)
    out = jnp.where(lane == 3, g2, out)
    o_ref[...] = out


def _router(x, w_router, *, tm=512):
    m, d = x.shape
    tm = _pick(m, tm)
    w = jnp.zeros((d, LANES), F32).at[:, :N_EXPERTS].set(w_router.astype(F32))
    return pl.pallas_call(
        _router_kernel,
        out_shape=jax.ShapeDtypeStruct((m, LANES), F32),
        grid=(m // tm,),
        in_specs=[pl.BlockSpec((tm, d), lambda i: (i, 0)), pl.BlockSpec((d, LANES), lambda i: (0, 0))],
        out_specs=pl.BlockSpec((tm, LANES), lambda i: (i, 0)),
        compiler_params=_cparams("parallel"), name="router",
    )(x, w)


def _row_copy(src_hbm, row, dst, dst_row, sem):
    return pltpu.make_async_copy(src_hbm.at[pl.ds(row, 1)], dst.at[pl.ds(dst_row, 1)], sem)


def _gather_kernel(tok_ref, x_hbm, o_ref, buf, sem, *, rows):
    base = pl.program_id(0) * rows

    def issue(t, c):
        _row_copy(x_hbm, tok_ref[base + t], buf, t, sem).start()
        return c

    lax.fori_loop(0, rows, issue, 0)

    def wait(t, c):
        _row_copy(x_hbm, 0, buf, t, sem).wait()
        return c

    lax.fori_loop(0, rows, wait, 0)
    o_ref[...] = buf[...].astype(o_ref.dtype)


def _gather_rows(x, slot_tok, *, rows=MOE_GROUP):
    n_slots = slot_tok.shape[0]
    d = x.shape[1]
    gs = pltpu.PrefetchScalarGridSpec(
        num_scalar_prefetch=1, grid=(n_slots // rows,),
        in_specs=[pl.BlockSpec(memory_space=pl.ANY)],
        out_specs=pl.BlockSpec((rows, d), lambda g, tok: (g, 0)),
        scratch_shapes=[pltpu.VMEM((rows, d), x.dtype), pltpu.SemaphoreType.DMA],
    )
    return pl.pallas_call(
        functools.partial(_gather_kernel, rows=rows),
        out_shape=jax.ShapeDtypeStruct((n_slots, d), BF16), grid_spec=gs,
        compiler_params=_cparams("arbitrary"), name="moe_gather",
    )(slot_tok, x)


def _combine_kernel(pos_ref, ys_hbm, x_ref, gate_ref, g_ref, b_ref, o32_ref, o16_ref, buf, sem, *, rows):
    base = pl.program_id(0) * rows

    def issue(t, c):
        for kx in range(TOP_K):
            _row_copy(ys_hbm, pos_ref[(base + t) * TOP_K + kx], buf.at[kx], t, sem).start()
        return c

    lax.fori_loop(0, rows, issue, 0)

    def wait(t, c):
        for kx in range(TOP_K):
            _row_copy(ys_hbm, 0, buf.at[kx], t, sem).wait()
        return c

    lax.fori_loop(0, rows, wait, 0)
    gate = gate_ref[...]
    moe = buf[0] * gate[:, 0:1] + buf[1] * gate[:, 1:2]
    out = _layer_norm_rows(DEEPNORM_ALPHA * x_ref[...] + moe, g_ref[...], b_ref[...])
    o32_ref[...] = out
    o16_ref[...] = out.astype(BF16)


def _combine_ln(ys, pos, gates, x, g, b, *, rows=128):
    m, d = x.shape
    rows = _pick(m, rows)
    row = lambda i, pos: (i, 0)
    gs = pltpu.PrefetchScalarGridSpec(
        num_scalar_prefetch=1, grid=(m // rows,),
        in_specs=[pl.BlockSpec(memory_space=pl.ANY),
                  pl.BlockSpec((rows, d), row),
                  pl.BlockSpec((rows, TOP_K), row),
                  pl.BlockSpec((1, d), lambda i, pos: (0, 0)),
                  pl.BlockSpec((1, d), lambda i, pos: (0, 0))],
        out_specs=(pl.BlockSpec((rows, d), row), pl.BlockSpec((rows, d), row)),
        scratch_shapes=[pltpu.VMEM((TOP_K, rows, d), ys.dtype), pltpu.SemaphoreType.DMA],
    )
    return pl.pallas_call(
        functools.partial(_combine_kernel, rows=rows),
        out_shape=(jax.ShapeDtypeStruct((m, d), F32), jax.ShapeDtypeStruct((m, d), BF16)),
        grid_spec=gs, compiler_params=_cparams("arbitrary"), name="moe_combine_ln",
    )(pos.reshape(-1), ys, x, gates, g.reshape(1, d), b.reshape(1, d))


def _moe_layer(x32, x16, w_router, w_gate_up, w_down, ln_g, ln_b):
    n_tok, d = x32.shape
    fe = w_down.shape[1]
    route = _router(x32, w_router)
    e_idx = route[:, :TOP_K].astype(jnp.int32)
    gates = route[:, TOP_K:2 * TOP_K]
    flat_e = e_idx.reshape(-1)
    n_asg = flat_e.shape[0]
    onehot = (flat_e[:, None] == jnp.arange(N_EXPERTS, dtype=jnp.int32)[None, :]).astype(jnp.int32)
    csum = jnp.cumsum(onehot, axis=0)
    rank = jnp.sum((csum - onehot) * onehot, axis=1)
    counts = csum[-1]
    padded = (counts + MOE_GROUP - 1) // MOE_GROUP * MOE_GROUP
    padded_end = jnp.cumsum(padded)
    pos = (padded_end - padded)[flat_e] + rank
    n_slots = -(-n_asg // MOE_GROUP) * MOE_GROUP + N_EXPERTS * MOE_GROUP
    n_groups = n_slots // MOE_GROUP
    flat_tok = jnp.repeat(jnp.arange(n_tok, dtype=jnp.int32), TOP_K)
    slot_tok = jnp.zeros((n_slots,), jnp.int32).at[pos].set(flat_tok, unique_indices=True)
    group_start = jnp.arange(n_groups, dtype=jnp.int32) * MOE_GROUP
    group_expert = jnp.minimum(jnp.sum(padded_end[None, :] <= group_start[:, None], axis=1),
                               N_EXPERTS - 1).astype(jnp.int32)

    xs = _gather_rows(x32, slot_tok)
    hs = _mm(xs, w_gate_up, n_out=fe, tm=MOE_GROUP, tn=512, swiglu=True, group_expert=group_expert)
    ys = _mm(hs, w_down, n_out=d, tm=MOE_GROUP, tn=512, out_dtype=F32, group_expert=group_expert)
    return _combine_ln(ys, pos.reshape(n_tok, TOP_K), gates, x32, ln_g, ln_b)


def _rope_tables(pos, dim):
    inv = ROPE_THETA ** (-jnp.arange(0, dim, 2, dtype=F32) / dim)
    ang = pos[:, None] * inv[None, :]
    ang = jnp.concatenate([ang, ang], axis=-1)
    sign = jnp.concatenate([-jnp.ones((dim // 2,), F32), jnp.ones((dim // 2,), F32)])
    return jnp.cos(ang), jnp.sin(ang) * sign[None, :]


def _lambda_init(layer_idx):
    return 0.8 - 0.6 * math.exp(-0.3 * layer_idx)


def kernel(x_prompt, x_sample, l0_w_qkv, l0_lambda_q1, l0_lambda_k1, l0_lambda_q2, l0_lambda_k2, l0_subln_g, l0_w_o, l0_ln1_g, l0_ln1_b, l0_ffn_w_gate_up, l0_ffn_w_down, l0_ln2_g, l0_ln2_b, l1_w_dqkv, l1_q_norm_g, l1_w_uq, l1_kv_norm_g, l1_w_ukv, l1_w_o, l1_ln1_g, l1_ln1_b, l1_router, l1_exp_w_gate_up, l1_exp_w_down, l1_ln2_g, l1_ln2_b, l2_w_qkv, l2_rpb, l2_w_o, l2_ln1_g, l2_ln1_b, l2_ffn_w_gate_up, l2_ffn_w_down, l2_ln2_g, l2_ln2_b, l3_w_qkv, l3_q_norm_g, l3_k_norm_g, l3_w_o, l3_ln1_g, l3_ln1_b, l3_router, l3_exp_w_gate_up, l3_exp_w_down, l3_ln2_g, l3_ln2_b):
    d = x_prompt.shape[-1]
    streams = ((x_prompt.shape[0], x_prompt.shape[1]), (x_sample.shape[0], x_sample.shape[1]))
    n_p = streams[0][0] * streams[0][1]
    n_s = streams[1][0] * streams[1][1]
    row0s = (0, n_p)
    n_heads = d // HEAD_DIM

    x32 = jnp.concatenate([x_prompt.reshape(n_p, d), x_sample.reshape(n_s, d)], axis=0)
    x16 = x32.astype(BF16)
    pos = jnp.concatenate([jnp.tile(jnp.arange(s, dtype=jnp.int32), b) for b, s in streams])
    posf = pos.astype(F32)

    def per_stream(fn):
        return jnp.concatenate([fn(row0, b, s) for row0, (b, s) in zip(row0s, streams)], axis=0)

    def dense_ffn(x32, x16, w_gate_up, w_down, g, b):
        f = w_down.shape[0]
        h = _mm(x16, w_gate_up.astype(BF16), n_out=f, tm=1024, tn=512, swiglu=True)
        y = _mm(h, w_down.astype(BF16), n_out=d, tm=256, tn=512, out_dtype=F32)
        return _residual_ln(x32, y, g, b)

    def attn_out(x32, o, w_o, g, b):
        y = _mm(o, w_o.astype(BF16), n_out=d, out_dtype=F32)
        return _residual_ln(x32, y, g, b)

    w = l0_w_qkv.astype(BF16)
    rope128 = _rope_tables(posf, HEAD_DIM)
    q = _mm(x16, w, n_out=d, col0=0, rope=rope128, rope_w=HEAD_DIM, scale=HEAD_DIM ** -0.5)
    k = _mm(x16, w, n_out=d, col0=d, rope=rope128, rope_w=HEAD_DIM)
    v = _mm(x16, w, n_out=d, col0=2 * d)
    lam0 = _lambda_init(0)
    diff_params = [p.astype(F32).reshape(1, -1) for p in
                   (l0_lambda_q1, l0_lambda_k1, l0_lambda_q2, l0_lambda_k2, l0_subln_g)]
    o = per_stream(lambda row0, b, s: _flash(
        "diff", q, k, v, row0=row0, batch=b, seq=s, n_groups=n_heads // 2,
        diff_params=diff_params, lam_init=lam0))
    x32, x16 = attn_out(x32, o, l0_w_o, l0_ln1_g, l0_ln1_b)
    x32, x16 = dense_ffn(x32, x16, l0_ffn_w_gate_up, l0_ffn_w_down, l0_ln2_g, l0_ln2_b)

    q_lora = l1_q_norm_g.shape[0]
    kv_lora = l1_kv_norm_g.shape[0]
    w = l1_w_dqkv.astype(BF16)
    c_q = _mm(x16, w, n_out=q_lora, col0=0, tn=q_lora,
              rms_gain=l1_q_norm_g.reshape(1, -1), rms_w=q_lora)
    c_kv = _mm(x16, w, n_out=kv_lora, col0=q_lora, tn=kv_lora,
               rms_gain=l1_kv_norm_g.reshape(1, -1), rms_w=kv_lora)
    cos64, sin64 = _rope_tables(posf, MLA_ROPE)
    n_tok = x32.shape[0]
    pad1 = jnp.ones((n_tok, HEAD_DIM - MLA_ROPE), F32)
    pad0 = jnp.zeros((n_tok, HEAD_DIM - MLA_ROPE), F32)
    kr_tab = (jnp.concatenate([cos64, pad1], axis=1), jnp.concatenate([sin64, pad0], axis=1))
    q_tab = (jnp.concatenate([jnp.ones((n_tok, MLA_NOPE), F32), kr_tab[0]], axis=1),
             jnp.concatenate([jnp.zeros((n_tok, MLA_NOPE), F32), kr_tab[1]], axis=1))
    w_kr = jnp.pad(w[:, q_lora + kv_lora:], ((0, 0), (0, HEAD_DIM - MLA_ROPE)))
    k_rope = _mm(x16, w_kr, n_out=HEAD_DIM, tn=HEAD_DIM, rope=kr_tab, rope_w=MLA_ROPE)
    w_uq = l1_w_uq.astype(BF16).reshape(q_lora, n_heads, MLA_NOPE + MLA_ROPE)
    w_uq = jnp.pad(w_uq, ((0, 0), (0, 0), (0, 2 * HEAD_DIM - MLA_NOPE - MLA_ROPE)))
    w_uq = w_uq.reshape(q_lora, n_heads * 2 * HEAD_DIM)
    q = _mm(c_q, w_uq, n_out=n_heads * 2 * HEAD_DIM, rope=q_tab, rope_w=MLA_ROPE,
            scale=(MLA_NOPE + MLA_ROPE) ** -0.5)
    kv = _mm(c_kv, l1_w_ukv.astype(BF16), n_out=n_heads * (MLA_NOPE + MLA_V))
    o = per_stream(lambda row0, b, s: _flash(
        "mla", q, kv, kv, kr=k_rope, row0=row0, batch=b, seq=s, n_groups=n_heads,
        k_col=lambda h: 2 * h, v_col=lambda h: 2 * h + 1, tq=1024))
    x32, x16 = attn_out(x32, o, l1_w_o, l1_ln1_g, l1_ln1_b)
    x32, x16 = _moe_layer(x32, x16, l1_router, l1_exp_w_gate_up.astype(BF16),
                          l1_exp_w_down.astype(BF16), l1_ln2_g, l1_ln2_b)

    qkv = _mm(x16, l2_w_qkv.astype(BF16), n_out=3 * d)
    bias = _na_bias_table(l2_rpb)
    o = per_stream(lambda row0, b, s: _neighbourhood(
        qkv, bias, row0=row0, batch=b, seq=s, n_heads=n_heads))
    x32, x16 = attn_out(x32, o, l2_w_o, l2_ln1_g, l2_ln1_b)
    x32, x16 = dense_ffn(x32, x16, l2_ffn_w_gate_up, l2_ffn_w_down, l2_ln2_g, l2_ln2_b)

    w = l3_w_qkv.astype(BF16)
    kv_w = GQA_KV_HEADS * HEAD_DIM
    half = HEAD_DIM // 2
    cr, sr = _rope_tables((pos // GRID_W).astype(F32), half)
    cc, sc = _rope_tables((pos % GRID_W).astype(F32), half)
    axial = (jnp.concatenate([cr, cc], axis=1), jnp.concatenate([sr, sc], axis=1))
    qg = jnp.tile(l3_q_norm_g.astype(F32), n_heads).reshape(1, -1)
    kg = jnp.tile(l3_k_norm_g.astype(F32), GQA_KV_HEADS).reshape(1, -1)
    q = _mm(x16, w, n_out=d, col0=0, rms_gain=qg, rms_w=HEAD_DIM, rope=axial, rope_w=half,
            scale=HEAD_DIM ** -0.5)
    k = _mm(x16, w, n_out=kv_w, col0=d, rms_gain=kg, rms_w=HEAD_DIM, rope=axial, rope_w=half)
    v = _mm(x16, w, n_out=kv_w, col0=d + kv_w)
    o = per_stream(lambda row0, b, s: _flash(
        "gqa", q, k, v, row0=row0, batch=b, seq=s, n_groups=GQA_KV_HEADS))
    x32, x16 = attn_out(x32, o, l3_w_o, l3_ln1_g, l3_ln1_b)
    x32, x16 = _moe_layer(x32, x16, l3_router, l3_exp_w_gate_up.astype(BF16),
                          l3_exp_w_down.astype(BF16), l3_ln2_g, l3_ln2_b)

    return (x32[:n_p].reshape(x_prompt.shape), x32[n_p:].reshape(x_sample.shape))
```

```python
import functools
import math

import numpy as np
import jax
import jax.numpy as jnp
from jax import lax
from jax.experimental import pallas as pl
from jax.experimental.pallas import tpu as pltpu

HEAD_DIM = 128
GRID_W = 64
ROPE_THETA = 10000.0
LN_EPS = 1e-5
RMS_EPS = 1e-6
DEPTH = 4
DEEPNORM_ALPHA = (2 * DEPTH) ** 0.25
MLA_NOPE = 128
MLA_ROPE = 64
MLA_V = 128
NA_WIN_ROWS = 8
NA_WIN_COLS = 16
GQA_KV_HEADS = 8
N_EXPERTS = 8
TOP_K = 2
MOE_GROUP = 256
MASK_VALUE = -1e30

V7X_VMEM_LIMIT_BYTES = 56 * 1024 * 1024
LANES = 128
FLASH_ROWS = 128
LOG2E = math.log2(math.e)

F32 = jnp.float32
BF16 = jnp.bfloat16


def _cparams(*sem):
    return pltpu.CompilerParams(dimension_semantics=sem, vmem_limit_bytes=V7X_VMEM_LIMIT_BYTES)


def _pick(n, pref):
    if n <= pref:
        return n
    t = pref
    while n % t:
        t //= 2
    return t


def _rot_half(xs, w, first):
    tw = xs.shape[1]
    if w == tw:
        return pltpu.roll(xs, w // 2, 1)
    return jnp.where(first, pltpu.roll(xs, tw - w // 2, 1), pltpu.roll(xs, w // 2, 1))


def _mm_kernel(*refs, expert, swiglu, rms_w, rope_w, scale, cast_b):
    refs = list(refs)
    ge_ref = refs.pop(0) if expert else None
    a_ref = refs.pop(0)
    b_ref = refs.pop(0)
    bu_ref = refs.pop(0) if swiglu else None
    gain_ref = refs.pop(0) if rms_w else None
    cos_ref = refs.pop(0) if rope_w else None
    sin_ref = refs.pop(0) if rope_w else None
    o_ref = refs.pop(0)
    i = pl.program_id(1)

    if cast_b:
        wb_ref = refs.pop(0)
        wu_ref = refs.pop(0) if swiglu else None
        changed = i == 0
        if expert:
            changed = changed | (ge_ref[i] != ge_ref[jnp.maximum(i - 1, 0)])

        @pl.when(changed)
        def _():
            wb_ref[...] = b_ref[...].astype(BF16)
            if swiglu:
                wu_ref[...] = bu_ref[...].astype(BF16)

        b_ref, bu_ref = wb_ref, wu_ref

    if expert:
        used = i < ge_ref[pl.num_programs(1)]

        @pl.when(jnp.logical_not(used))
        def _():
            o_ref[...] = jnp.zeros_like(o_ref)

        pl.when(used)(functools.partial(
            _mm_compute, a_ref, b_ref, bu_ref, gain_ref, cos_ref, sin_ref, o_ref,
            swiglu=swiglu, rms_w=rms_w, rope_w=rope_w, scale=scale))
    else:
        _mm_compute(a_ref, b_ref, bu_ref, gain_ref, cos_ref, sin_ref, o_ref,
                    swiglu=swiglu, rms_w=rms_w, rope_w=rope_w, scale=scale)


def _mm_compute(a_ref, b_ref, bu_ref, gain_ref, cos_ref, sin_ref, o_ref, *, swiglu, rms_w, rope_w, scale):
    a = a_ref[...]
    acc = jnp.dot(a, b_ref[...], preferred_element_type=F32)
    if swiglu:
        up = jnp.dot(a, bu_ref[...], preferred_element_type=F32)
        acc = acc * jax.nn.sigmoid(acc) * up
    tm, tn = acc.shape
    if rms_w:
        parts = []
        for c in range(tn // rms_w):
            xs = acc[:, c * rms_w:(c + 1) * rms_w]
            ms = jnp.mean(xs * xs, axis=-1, keepdims=True)
            parts.append(xs * lax.rsqrt(ms + RMS_EPS) * gain_ref[:, c * rms_w:(c + 1) * rms_w])
        acc = parts[0] if len(parts) == 1 else jnp.concatenate(parts, axis=1)
    if rope_w:
        tw = cos_ref.shape[1]
        cos = cos_ref[...]
        sin = sin_ref[...]
        lane = lax.broadcasted_iota(jnp.int32, (tm, tw), 1)
        first = (lane % rope_w) < (rope_w // 2)
        for c in range(tn // tw):
            xs = acc[:, c * tw:(c + 1) * tw]
            xs = xs * cos + _rot_half(xs, rope_w, first) * sin
            if scale != 1.0:
                xs = xs * scale
            o_ref[:, c * tw:(c + 1) * tw] = xs.astype(o_ref.dtype)
    else:
        if scale != 1.0:
            acc = acc * scale
        o_ref[...] = acc.astype(o_ref.dtype)


def _mm(a, b, *, n_out, col0=0, tm=1024, tn=512, out_dtype=BF16, swiglu=False,
        group_expert=None, rms_gain=None, rms_w=None, rope=None, rope_w=None, scale=1.0):
    m, k = a.shape
    tm = _pick(m, tm)
    tn = _pick(n_out, tn)
    assert m % tm == 0 and n_out % tn == 0 and col0 % tn == 0
    expert = group_expert is not None
    cb = col0 // tn
    ub = (b.shape[-1] // 2) // tn
    grid = (n_out // tn, m // tm)

    if expert:
        a_map = lambda j, i, ge: (i, 0)
        b_map = lambda j, i, ge: (ge[i], 0, cb + j)
        u_map = lambda j, i, ge: (ge[i], 0, cb + ub + j)
        c_map = lambda j, i, ge: (0, j)
        t_map = lambda j, i, ge: (i, 0)
        o_map = lambda j, i, ge: (i, j)
        b_block = (None, k, tn)
    else:
        a_map = lambda j, i: (i, 0)
        b_map = lambda j, i: (0, cb + j)
        u_map = lambda j, i: (0, cb + ub + j)
        c_map = lambda j, i: (0, j)
        t_map = lambda j, i: (i, 0)
        o_map = lambda j, i: (i, j)
        b_block = (k, tn)

    in_specs = [pl.BlockSpec((tm, k), a_map), pl.BlockSpec(b_block, b_map)]
    args = [a, b]
    if swiglu:
        assert b.shape[-1] % (2 * tn) == 0
        in_specs.append(pl.BlockSpec(b_block, u_map))
        args.append(b)
    if rms_w:
        assert tn % rms_w == 0
        in_specs.append(pl.BlockSpec((1, tn), c_map))
        args.append(rms_gain)
    if rope_w:
        cos, sin = rope
        tw = cos.shape[1]
        assert tn % tw == 0 and cos.shape[0] == m
        in_specs += [pl.BlockSpec((tm, tw), t_map), pl.BlockSpec((tm, tw), t_map)]
        args += [cos, sin]

    cast_b = b.dtype != BF16
    scratch = [pltpu.VMEM((k, tn), BF16)] * ((2 if swiglu else 1) if cast_b else 0)
    sem = ("parallel", "arbitrary") if cast_b else ("parallel", "parallel")
    kern = functools.partial(_mm_kernel, expert=expert, swiglu=swiglu, rms_w=rms_w,
                             rope_w=rope_w, scale=float(scale), cast_b=cast_b)
    out_shape = jax.ShapeDtypeStruct((m, n_out), out_dtype)
    out_spec = pl.BlockSpec((tm, tn), o_map)
    if expert:
        assert group_expert.shape[0] == m // tm + 1
        gs = pltpu.PrefetchScalarGridSpec(num_scalar_prefetch=1, grid=grid, in_specs=in_specs,
                                          out_specs=out_spec, scratch_shapes=scratch)
        return pl.pallas_call(kern, out_shape=out_shape, grid_spec=gs,
                              compiler_params=_cparams(*sem), name="expert_mm")(group_expert, *args)
    return pl.pallas_call(kern, out_shape=out_shape, grid=grid, in_specs=in_specs,
                          out_specs=out_spec, scratch_shapes=scratch,
                          compiler_params=_cparams(*sem), name="mm")(*args)


def _layer_norm_rows(z, g, b):
    mu = jnp.mean(z, axis=-1, keepdims=True)
    zc = z - mu
    var = jnp.mean(zc * zc, axis=-1, keepdims=True)
    return zc * lax.rsqrt(var + LN_EPS) * g + b


def _ln_kernel(x_ref, y_ref, g_ref, b_ref, o32_ref, o16_ref):
    z = DEEPNORM_ALPHA * x_ref[...] + y_ref[...].astype(F32)
    out = _layer_norm_rows(z, g_ref[...], b_ref[...])
    o32_ref[...] = out
    o16_ref[...] = out.astype(BF16)


def _residual_ln(x, y, g, b, *, tm=256):
    m, d = x.shape
    tm = _pick(m, tm)
    row = pl.BlockSpec((tm, d), lambda i: (i, 0))
    vec = pl.BlockSpec((1, d), lambda i: (0, 0))
    return pl.pallas_call(
        _ln_kernel,
        out_shape=(jax.ShapeDtypeStruct((m, d), F32), jax.ShapeDtypeStruct((m, d), BF16)),
        grid=(m // tm,), in_specs=[row, row, vec, vec], out_specs=(row, row),
        compiler_params=_cparams("parallel"), name="residual_ln",
    )(x, y, g.reshape(1, d), b.reshape(1, d))


def _flash_kernel(*refs, mode, n_rep, nk, lam_init):
    refs = list(refs)
    q_ref = refs.pop(0)
    k_ref = refs.pop(0)
    kr_ref = refs.pop(0) if mode == "mla" else None
    v_ref = refs.pop(0)
    if mode == "diff":
        lq1_ref, lk1_ref, lq2_ref, lk2_ref, sg_ref = [refs.pop(0) for _ in range(5)]
    o_ref = refs.pop(0)
    m_s, l_s, acc_s, s_s, p_s = refs

    kk = pl.program_id(3)

    @pl.when(kk == 0)
    def _():
        m_s[...] = jnp.full_like(m_s, -jnp.inf)
        l_s[...] = jnp.zeros_like(l_s)
        acc_s[...] = jnp.zeros_like(acc_s)

    q = q_ref[...]
    k = k_ref[...]
    if mode == "mla":
        k = jnp.concatenate([k, kr_ref[...]], axis=1)
    v = v_ref[...]
    tq, tk = s_s.shape[1], s_s.shape[2]
    dq = q.shape[1] // n_rep
    dv = acc_s.shape[2]
    for r in range(n_rep):
        q_r = q[:, r * dq:(r + 1) * dq]
        k_r = k[:, r * dq:(r + 1) * dq] if mode == "diff" else k
        s_s[r] = lax.dot_general(q_r, k_r, (((1,), (1,)), ((), ())), preferred_element_type=F32)
        for rb in range(tq // FLASH_ROWS):
            rows = slice(rb * FLASH_ROWS, (rb + 1) * FLASH_ROWS)
            mp = s_s[r, rows, 0:LANES]
            for c in range(1, tk // LANES):
                mp = jnp.maximum(mp, s_s[r, rows, c * LANES:(c + 1) * LANES])
            m_prev = m_s[r, rows, :]
            m_new = jnp.maximum(m_prev, jnp.max(mp, axis=-1, keepdims=True))
            alpha = jnp.exp2(m_prev - m_new)
            lp = jnp.zeros((FLASH_ROWS, LANES), F32)
            for c in range(tk // LANES):
                p = jnp.exp2(s_s[r, rows, c * LANES:(c + 1) * LANES] - m_new)
                lp = lp + p
                p_s[r, rows, c * LANES:(c + 1) * LANES] = p.astype(BF16)
            l_s[r, rows, :] = alpha * l_s[r, rows, :] + jnp.sum(lp, axis=-1, keepdims=True)
            m_s[r, rows, :] = m_new
            for c in range(dv // LANES):
                cols = slice(c * LANES, (c + 1) * LANES)
                acc_s[r, rows, cols] = acc_s[r, rows, cols] * alpha
        acc_s[r] += jnp.dot(p_s[r], v, preferred_element_type=F32)

    def normalised(r):
        acc, l = acc_s[r], l_s[r]
        parts = [acc[:, c * LANES:(c + 1) * LANES] / l for c in range(dv // LANES)]
        return parts[0] if len(parts) == 1 else jnp.concatenate(parts, axis=1)

    @pl.when(kk == nk - 1)
    def _():
        if mode == "diff":
            lam = (jnp.exp(jnp.sum(lq1_ref[...] * lk1_ref[...], axis=-1, keepdims=True))
                   - jnp.exp(jnp.sum(lq2_ref[...] * lk2_ref[...], axis=-1, keepdims=True)) + lam_init)
            o = normalised(0) - lam * normalised(1)
            ms = jnp.mean(o * o, axis=-1, keepdims=True)
            o = o * lax.rsqrt(ms + RMS_EPS) * sg_ref[...] * (1.0 - lam_init)
            o_ref[...] = o.astype(o_ref.dtype)
        else:
            for r in range(n_rep):
                o_ref[:, r * dv:(r + 1) * dv] = normalised(r).astype(o_ref.dtype)


def _flash(mode, q, k, v, *, row0, batch, seq, n_groups, kr=None, diff_params=None,
           lam_init=0.0, k_col=None, v_col=None, tq=512, tk=1024):
    tq = _pick(seq, tq)
    tk = _pick(seq, tk)
    nq, nk = seq // tq, seq // tk
    rq, rk = row0 // tq, row0 // tk
    assert row0 % tq == 0 and row0 % tk == 0
    if mode == "diff":
        n_rep, qw, kw, vw, ow = 2, 2 * HEAD_DIM, 2 * HEAD_DIM, 2 * HEAD_DIM, 2 * HEAD_DIM
    elif mode == "gqa":
        n_rep = q.shape[1] // (n_groups * HEAD_DIM)
        qw, kw, vw, ow = n_rep * HEAD_DIM, HEAD_DIM, HEAD_DIM, n_rep * HEAD_DIM
    else:
        n_rep, qw, kw, vw, ow = 1, 2 * HEAD_DIM, HEAD_DIM, HEAD_DIM, HEAD_DIM
    k_col = k_col or (lambda h: h)
    v_col = v_col or (lambda h: h)

    in_specs = [
        pl.BlockSpec((tq, qw), lambda b, h, i, kk: (rq + b * nq + i, h)),
        pl.BlockSpec((tk, kw), lambda b, h, i, kk: (rk + b * nk + kk, k_col(h))),
    ]
    args = [q, k]
    if mode == "mla":
        in_specs.append(pl.BlockSpec((tk, HEAD_DIM), lambda b, h, i, kk: (rk + b * nk + kk, 0)))
        args.append(kr)
    in_specs.append(pl.BlockSpec((tk, vw), lambda b, h, i, kk: (rk + b * nk + kk, v_col(h))))
    args.append(v)
    if mode == "diff":
        for p in diff_params:
            in_specs.append(pl.BlockSpec((1, p.shape[1]), lambda b, h, i, kk: (0, 0)))
            args.append(p)
    dv = vw
    kern = functools.partial(_flash_kernel, mode=mode, n_rep=n_rep, nk=nk, lam_init=lam_init)
    return pl.pallas_call(
        kern,
        out_shape=jax.ShapeDtypeStruct((batch * seq, n_groups * ow), BF16),
        grid=(batch, n_groups, nq, nk),
        in_specs=in_specs,
        out_specs=pl.BlockSpec((tq, ow), lambda b, h, i, kk: (b * nq + i, h)),
        scratch_shapes=[pltpu.VMEM((n_rep, tq, LANES), F32), pltpu.VMEM((n_rep, tq, LANES), F32),
                        pltpu.VMEM((n_rep, tq, dv), F32),
                        pltpu.VMEM((n_rep, tq, tk), F32), pltpu.VMEM((n_rep, tq, tk), BF16)],
        compiler_params=_cparams("parallel", "parallel", "parallel", "arbitrary"),
        name="flash_" + mode,
    )(*args)


def _na_kernel(q_ref, k_ref, v_ref, *rest, n_rows, heads, scale, rows_per_step):
    bias_refs, o_ref = rest[:rows_per_step], rest[rows_per_step]
    for t in range(rows_per_step):
        r = pl.program_id(2) * rows_per_step + t
        rs = jnp.clip(r - NA_WIN_ROWS // 2, 0, n_rows - NA_WIN_ROWS)
        start = pl.multiple_of(rs * GRID_W, GRID_W)
        kb = k_ref[pl.ds(start, NA_WIN_ROWS * GRID_W), :]
        vb = v_ref[pl.ds(start, NA_WIN_ROWS * GRID_W), :]
        qrows = slice(t * GRID_W, (t + 1) * GRID_W)
        for h in range(heads):
            sl = slice(h * HEAD_DIM, (h + 1) * HEAD_DIM)
            s = lax.dot_general(q_ref[qrows, sl], kb[:, sl], (((1,), (1,)), ((), ())),
                                preferred_element_type=F32)
            s = s * scale + bias_refs[t][h]
            m = jnp.max(s, axis=-1, keepdims=True)
            p = jnp.exp(s - m)
            l = jnp.sum(p, axis=-1, keepdims=True)
            o = jnp.dot(p.astype(BF16), vb[:, sl], preferred_element_type=F32) / l
            o_ref[qrows, sl] = o.astype(o_ref.dtype)


def _na_bias_table(rpb):
    c = np.arange(GRID_W)
    cs = np.clip(c - NA_WIN_COLS // 2, 0, GRID_W - NA_WIN_COLS)
    col_in = (c[None, :] >= cs[:, None]) & (c[None, :] < cs[:, None] + NA_WIN_COLS)
    dc_idx = np.clip(c[None, :] - c[:, None], -(NA_WIN_COLS - 1), NA_WIN_COLS - 1) + NA_WIN_COLS - 1
    case = np.arange(NA_WIN_ROWS)
    dr_idx = np.arange(NA_WIN_ROWS)[None, :] - case[:, None] + NA_WIN_ROWS - 1
    rows = rpb.astype(F32)[:, dr_idx, :]
    onehot = (dc_idx[:, :, None] == np.arange(2 * NA_WIN_COLS - 1)).astype(np.float32)
    t = jnp.einsum("hcjd,qkd->hcqjk", rows, onehot, precision=lax.Precision.HIGHEST)
    t = jnp.where(col_in[None, None, :, None, :], t, MASK_VALUE)
    return t.reshape(rpb.shape[0], NA_WIN_ROWS, GRID_W, NA_WIN_ROWS * GRID_W)


def _neighbourhood(qkv, bias, *, row0, batch, seq, n_heads, heads_per_step=4, rows_per_step=4):
    n_rows = seq // GRID_W
    assert n_rows >= NA_WIN_ROWS and n_rows % rows_per_step == 0
    hb = heads_per_step
    rps = rows_per_step
    nhb = n_heads // hb
    n_steps = n_rows // rps
    r0 = row0 // (GRID_W * rps)
    b0 = row0 // seq
    assert row0 % seq == 0

    def bias_map(t):
        def index(b, h, i):
            r = i * rps + t
            rs = jnp.clip(r - NA_WIN_ROWS // 2, 0, n_rows - NA_WIN_ROWS)
            return (h, r - rs, 0, 0)
        return index

    kern = functools.partial(_na_kernel, n_rows=n_rows, heads=hb, scale=HEAD_DIM ** -0.5,
                             rows_per_step=rps)
    bias_specs = [pl.BlockSpec((hb, None, GRID_W, NA_WIN_ROWS * GRID_W), bias_map(t)) for t in range(rps)]
    return pl.pallas_call(
        kern,
        out_shape=jax.ShapeDtypeStruct((batch * seq, n_heads * HEAD_DIM), BF16),
        grid=(batch, nhb, n_steps),
        in_specs=[
            pl.BlockSpec((rps * GRID_W, hb * HEAD_DIM), lambda b, h, i: (r0 + b * n_steps + i, h)),
            pl.BlockSpec((seq, hb * HEAD_DIM), lambda b, h, i: (b0 + b, nhb + h)),
            pl.BlockSpec((seq, hb * HEAD_DIM), lambda b, h, i: (b0 + b, 2 * nhb + h)),
        ] + bias_specs,
        out_specs=pl.BlockSpec((rps * GRID_W, hb * HEAD_DIM), lambda b, h, i: (b * n_steps + i, h)),
        compiler_params=_cparams("parallel", "parallel", "arbitrary"),
        name="neighbourhood_attention",
    )(qkv, qkv, qkv, *([bias] * rps))


def _router_kernel(x_ref, w_ref, o_ref):
    logits = jnp.dot(x_ref[...], w_ref[...], preferred_element_type=F32,
                     precision=lax.Precision.HIGHEST)
    lane = lax.broadcasted_iota(jnp.int32, logits.shape, 1)
    logits = jnp.where(lane < N_EXPERTS, logits, -jnp.inf)
    m1 = jnp.max(logits, axis=-1, keepdims=True)
    i1 = jnp.min(jnp.where(logits == m1, lane, LANES), axis=-1, keepdims=True)
    rest = jnp.where(lane == i1, -jnp.inf, logits)
    m2 = jnp.max(rest, axis=-1, keepdims=True)
    i2 = jnp.min(jnp.where(rest == m2, lane, LANES), axis=-1, keepdims=True)
    e2 = jnp.exp(m2 - m1)
    g1 = 1.0 / (1.0 + e2)
    g2 = e2 / (1.0 + e2)
    out = jnp.where(lane == 0, i1.astype(F32), 0.0)
    out = jnp.where(lane == 1, i2.astype(F32), out)
    out = jnp.where(lane == 2, g1, out)
    out = jnp.where(lane == 3, g2, out)
    o_ref[...] = out


def _router(x, w_router, *, tm=512):
    m, d = x.shape
    tm = _pick(m, tm)
    w = jnp.zeros((d, LANES), F32).at[:, :N_EXPERTS].set(w_router.astype(F32))
    return pl.pallas_call(
        _router_kernel,
        out_shape=jax.ShapeDtypeStruct((m, LANES), F32),
        grid=(m // tm,),
        in_specs=[pl.BlockSpec((tm, d), lambda i: (i, 0)), pl.BlockSpec((d, LANES), lambda i: (0, 0))],
        out_specs=pl.BlockSpec((tm, LANES), lambda i: (i, 0)),
        compiler_params=_cparams("parallel"), name="router",
    )(x, w)


def _row_copy(src_hbm, row, dst, dst_row, sem):
    return pltpu.make_async_copy(src_hbm.at[pl.ds(row, 1)], dst.at[pl.ds(dst_row, 1)], sem)


def _gather_kernel(tok_ref, x_hbm, o_ref, buf, sem, *, rows):
    base = pl.program_id(0) * rows

    def issue(t, c):
        _row_copy(x_hbm, tok_ref[base + t], buf, t, sem).start()
        return c

    lax.fori_loop(0, rows, issue, 0)

    def wait(t, c):
        _row_copy(x_hbm, 0, buf, t, sem).wait()
        return c

    lax.fori_loop(0, rows, wait, 0)
    o_ref[...] = buf[...].astype(o_ref.dtype)


def _gather_rows(x, slot_tok, *, rows=MOE_GROUP):
    n_slots = slot_tok.shape[0]
    d = x.shape[1]
    gs = pltpu.PrefetchScalarGridSpec(
        num_scalar_prefetch=1, grid=(n_slots // rows,),
        in_specs=[pl.BlockSpec(memory_space=pl.ANY)],
        out_specs=pl.BlockSpec((rows, d), lambda g, tok: (g, 0)),
        scratch_shapes=[pltpu.VMEM((rows, d), x.dtype), pltpu.SemaphoreType.DMA],
    )
    return pl.pallas_call(
        functools.partial(_gather_kernel, rows=rows),
        out_shape=jax.ShapeDtypeStruct((n_slots, d), BF16), grid_spec=gs,
        compiler_params=_cparams("arbitrary"), name="moe_gather",
    )(slot_tok, x)


def _combine_kernel(pos_ref, ys_hbm, x_ref, gate_ref, g_ref, b_ref, o32_ref, o16_ref, buf, sem, *, rows):
    base = pl.program_id(0) * rows

    def issue(t, c):
        for kx in range(TOP_K):
            _row_copy(ys_hbm, pos_ref[(base + t) * TOP_K + kx], buf.at[kx], t, sem).start()
        return c

    lax.fori_loop(0, rows, issue, 0)

    def wait(t, c):
        for kx in range(TOP_K):
            _row_copy(ys_hbm, 0, buf.at[kx], t, sem).wait()
        return c

    lax.fori_loop(0, rows, wait, 0)
    gate = gate_ref[...]
    moe = buf[0] * gate[:, 0:1] + buf[1] * gate[:, 1:2]
    out = _layer_norm_rows(DEEPNORM_ALPHA * x_ref[...] + moe, g_ref[...], b_ref[...])
    o32_ref[...] = out
    o16_ref[...] = out.astype(BF16)


def _combine_ln(ys, pos, gates, x, g, b, *, rows=128):
    m, d = x.shape
    rows = _pick(m, rows)
    row = lambda i, pos: (i, 0)
    gs = pltpu.PrefetchScalarGridSpec(
        num_scalar_prefetch=1, grid=(m // rows,),
        in_specs=[pl.BlockSpec(memory_space=pl.ANY),
                  pl.BlockSpec((rows, d), row),
                  pl.BlockSpec((rows, TOP_K), row),
                  pl.BlockSpec((1, d), lambda i, pos: (0, 0)),
                  pl.BlockSpec((1, d), lambda i, pos: (0, 0))],
        out_specs=(pl.BlockSpec((rows, d), row), pl.BlockSpec((rows, d), row)),
        scratch_shapes=[pltpu.VMEM((TOP_K, rows, d), ys.dtype), pltpu.SemaphoreType.DMA],
    )
    return pl.pallas_call(
        functools.partial(_combine_kernel, rows=rows),
        out_shape=(jax.ShapeDtypeStruct((m, d), F32), jax.ShapeDtypeStruct((m, d), BF16)),
        grid_spec=gs, compiler_params=_cparams("arbitrary"), name="moe_combine_ln",
    )(pos.reshape(-1), ys, x, gates, g.reshape(1, d), b.reshape(1, d))


def _moe_layer(x32, x16, w_router, w_gate_up, w_down, ln_g, ln_b):
    n_tok, d = x32.shape
    fe = w_down.shape[1]
    route = _router(x32, w_router)
    e_idx = route[:, :TOP_K].astype(jnp.int32)
    gates = route[:, TOP_K:2 * TOP_K]
    flat_e = e_idx.reshape(-1)
    n_asg = flat_e.shape[0]
    onehot = (flat_e[:, None] == jnp.arange(N_EXPERTS, dtype=jnp.int32)[None, :]).astype(jnp.int32)
    csum = jnp.cumsum(onehot, axis=0)
    rank = jnp.sum((csum - onehot) * onehot, axis=1)
    counts = csum[-1]
    padded = (counts + MOE_GROUP - 1) // MOE_GROUP * MOE_GROUP
    padded_end = jnp.cumsum(padded)
    pos = (padded_end - padded)[flat_e] + rank
    n_slots = -(-n_asg // MOE_GROUP) * MOE_GROUP + N_EXPERTS * MOE_GROUP
    n_groups = n_slots // MOE_GROUP
    flat_tok = jnp.repeat(jnp.arange(n_tok, dtype=jnp.int32), TOP_K)
    slot_tok = jnp.zeros((n_slots,), jnp.int32).at[pos].set(flat_tok, unique_indices=True)
    group_start = jnp.arange(n_groups, dtype=jnp.int32) * MOE_GROUP
    group_expert = jnp.minimum(jnp.sum(padded_end[None, :] <= group_start[:, None], axis=1),
                               N_EXPERTS - 1).astype(jnp.int32)
    n_used = (padded_end[-1] // MOE_GROUP).astype(jnp.int32)
    group_expert = jnp.concatenate([group_expert, n_used[None]])

    xs = _gather_rows(x32, slot_tok)
    hs = _mm(xs, w_gate_up, n_out=fe, tm=MOE_GROUP, tn=512, swiglu=True, group_expert=group_expert)
    ys = _mm(hs, w_down, n_out=d, tm=MOE_GROUP, tn=512, out_dtype=F32, group_expert=group_expert)
    return _combine_ln(ys, pos.reshape(n_tok, TOP_K), gates, x32, ln_g, ln_b)


def _rope_tables(pos, dim):
    inv = ROPE_THETA ** (-jnp.arange(0, dim, 2, dtype=F32) / dim)
    ang = pos[:, None] * inv[None, :]
    ang = jnp.concatenate([ang, ang], axis=-1)
    sign = jnp.concatenate([-jnp.ones((dim // 2,), F32), jnp.ones((dim // 2,), F32)])
    return jnp.cos(ang), jnp.sin(ang) * sign[None, :]


def _lambda_init(layer_idx):
    return 0.8 - 0.6 * math.exp(-0.3 * layer_idx)


def kernel(x_prompt, x_sample, l0_w_qkv, l0_lambda_q1, l0_lambda_k1, l0_lambda_q2, l0_lambda_k2, l0_subln_g, l0_w_o, l0_ln1_g, l0_ln1_b, l0_ffn_w_gate_up, l0_ffn_w_down, l0_ln2_g, l0_ln2_b, l1_w_dqkv, l1_q_norm_g, l1_w_uq, l1_kv_norm_g, l1_w_ukv, l1_w_o, l1_ln1_g, l1_ln1_b, l1_router, l1_exp_w_gate_up, l1_exp_w_down, l1_ln2_g, l1_ln2_b, l2_w_qkv, l2_rpb, l2_w_o, l2_ln1_g, l2_ln1_b, l2_ffn_w_gate_up, l2_ffn_w_down, l2_ln2_g, l2_ln2_b, l3_w_qkv, l3_q_norm_g, l3_k_norm_g, l3_w_o, l3_ln1_g, l3_ln1_b, l3_router, l3_exp_w_gate_up, l3_exp_w_down, l3_ln2_g, l3_ln2_b):
    d = x_prompt.shape[-1]
    streams = ((x_prompt.shape[0], x_prompt.shape[1]), (x_sample.shape[0], x_sample.shape[1]))
    n_p = streams[0][0] * streams[0][1]
    n_s = streams[1][0] * streams[1][1]
    row0s = (0, n_p)
    n_heads = d // HEAD_DIM

    x32 = jnp.concatenate([x_prompt.reshape(n_p, d), x_sample.reshape(n_s, d)], axis=0)
    x16 = x32.astype(BF16)
    pos = jnp.concatenate([jnp.tile(jnp.arange(s, dtype=jnp.int32), b) for b, s in streams])
    posf = pos.astype(F32)

    def per_stream(fn):
        return jnp.concatenate([fn(row0, b, s) for row0, (b, s) in zip(row0s, streams)], axis=0)

    def dense_ffn(x32, x16, w_gate_up, w_down, g, b):
        f = w_down.shape[0]
        h = _mm(x16, w_gate_up.astype(BF16), n_out=f, tm=1024, tn=512, swiglu=True)
        y = _mm(h, w_down.astype(BF16), n_out=d, tm=256, tn=512, out_dtype=F32)
        return _residual_ln(x32, y, g, b)

    def attn_out(x32, o, w_o, g, b):
        y = _mm(o, w_o.astype(BF16), n_out=d, out_dtype=F32)
        return _residual_ln(x32, y, g, b)

    w = l0_w_qkv.astype(BF16)
    rope128 = _rope_tables(posf, HEAD_DIM)
    q = _mm(x16, w, n_out=d, col0=0, rope=rope128, rope_w=HEAD_DIM, scale=HEAD_DIM ** -0.5 * LOG2E)
    k = _mm(x16, w, n_out=d, col0=d, rope=rope128, rope_w=HEAD_DIM)
    v = _mm(x16, w, n_out=d, col0=2 * d)
    lam0 = _lambda_init(0)
    diff_params = [p.astype(F32).reshape(1, -1) for p in
                   (l0_lambda_q1, l0_lambda_k1, l0_lambda_q2, l0_lambda_k2, l0_subln_g)]
    o = per_stream(lambda row0, b, s: _flash(
        "diff", q, k, v, row0=row0, batch=b, seq=s, n_groups=n_heads // 2,
        diff_params=diff_params, lam_init=lam0, tq=1024))
    x32, x16 = attn_out(x32, o, l0_w_o, l0_ln1_g, l0_ln1_b)
    x32, x16 = dense_ffn(x32, x16, l0_ffn_w_gate_up, l0_ffn_w_down, l0_ln2_g, l0_ln2_b)

    q_lora = l1_q_norm_g.shape[0]
    kv_lora = l1_kv_norm_g.shape[0]
    w = l1_w_dqkv.astype(BF16)
    c_q = _mm(x16, w, n_out=q_lora, col0=0, tn=q_lora,
              rms_gain=l1_q_norm_g.reshape(1, -1), rms_w=q_lora)
    c_kv = _mm(x16, w, n_out=kv_lora, col0=q_lora, tn=kv_lora,
               rms_gain=l1_kv_norm_g.reshape(1, -1), rms_w=kv_lora)
    cos64, sin64 = _rope_tables(posf, MLA_ROPE)
    n_tok = x32.shape[0]
    pad1 = jnp.ones((n_tok, HEAD_DIM - MLA_ROPE), F32)
    pad0 = jnp.zeros((n_tok, HEAD_DIM - MLA_ROPE), F32)
    kr_tab = (jnp.concatenate([cos64, pad1], axis=1), jnp.concatenate([sin64, pad0], axis=1))
    q_tab = (jnp.concatenate([jnp.ones((n_tok, MLA_NOPE), F32), kr_tab[0]], axis=1),
             jnp.concatenate([jnp.zeros((n_tok, MLA_NOPE), F32), kr_tab[1]], axis=1))
    w_kr = jnp.pad(w[:, q_lora + kv_lora:], ((0, 0), (0, HEAD_DIM - MLA_ROPE)))
    k_rope = _mm(x16, w_kr, n_out=HEAD_DIM, tn=HEAD_DIM, rope=kr_tab, rope_w=MLA_ROPE)
    w_uq = l1_w_uq.astype(BF16).reshape(q_lora, n_heads, MLA_NOPE + MLA_ROPE)
    w_uq = jnp.pad(w_uq, ((0, 0), (0, 0), (0, 2 * HEAD_DIM - MLA_NOPE - MLA_ROPE)))
    w_uq = w_uq.reshape(q_lora, n_heads * 2 * HEAD_DIM)
    q = _mm(c_q, w_uq, n_out=n_heads * 2 * HEAD_DIM, rope=q_tab, rope_w=MLA_ROPE,
            scale=(MLA_NOPE + MLA_ROPE) ** -0.5 * LOG2E)
    kv = _mm(c_kv, l1_w_ukv.astype(BF16), n_out=n_heads * (MLA_NOPE + MLA_V))
    o = per_stream(lambda row0, b, s: _flash(
        "mla", q, kv, kv, kr=k_rope, row0=row0, batch=b, seq=s, n_groups=n_heads,
        k_col=lambda h: 2 * h, v_col=lambda h: 2 * h + 1, tq=1024))
    x32, x16 = attn_out(x32, o, l1_w_o, l1_ln1_g, l1_ln1_b)
    x32, x16 = _moe_layer(x32, x16, l1_router, l1_exp_w_gate_up.astype(F32),
                          l1_exp_w_down.astype(BF16), l1_ln2_g, l1_ln2_b)

    qkv = _mm(x16, l2_w_qkv.astype(BF16), n_out=3 * d)
    bias = _na_bias_table(l2_rpb)
    o = per_stream(lambda row0, b, s: _neighbourhood(
        qkv, bias, row0=row0, batch=b, seq=s, n_heads=n_heads))
    x32, x16 = attn_out(x32, o, l2_w_o, l2_ln1_g, l2_ln1_b)
    x32, x16 = dense_ffn(x32, x16, l2_ffn_w_gate_up, l2_ffn_w_down, l2_ln2_g, l2_ln2_b)

    w = l3_w_qkv.astype(BF16)
    kv_w = GQA_KV_HEADS * HEAD_DIM
    half = HEAD_DIM // 2
    cr, sr = _rope_tables((pos // GRID_W).astype(F32), half)
    cc, sc = _rope_tables((pos % GRID_W).astype(F32), half)
    axial = (jnp.concatenate([cr, cc], axis=1), jnp.concatenate([sr, sc], axis=1))
    qg = jnp.tile(l3_q_norm_g.astype(F32), n_heads).reshape(1, -1)
    kg = jnp.tile(l3_k_norm_g.astype(F32), GQA_KV_HEADS).reshape(1, -1)
    q = _mm(x16, w, n_out=d, col0=0, rms_gain=qg, rms_w=HEAD_DIM, rope=axial, rope_w=half,
            scale=HEAD_DIM ** -0.5 * LOG2E)
    k = _mm(x16, w, n_out=kv_w, col0=d, rms_gain=kg, rms_w=HEAD_DIM, rope=axial, rope_w=half)
    v = _mm(x16, w, n_out=kv_w, col0=d + kv_w)
    o = per_stream(lambda row0, b, s: _flash(
        "gqa", q, k, v, row0=row0, batch=b, seq=s, n_groups=GQA_KV_HEADS))
    x32, x16 = attn_out(x32, o, l3_w_o, l3_ln1_g, l3_ln1_b)
    x32, x16 = _moe_layer(x32, x16, l3_router, l3_exp_w_gate_up.astype(F32),
                          l3_exp_w_down.astype(BF16), l3_ln2_g, l3_ln2_b)

    return (x32[:n_p].reshape(x_prompt.shape), x32[n_p:].reshape(x_sample.shape))
```

```python
import functools
import math

import numpy as np
import jax
import jax.numpy as jnp
from jax import lax
from jax.experimental import pallas as pl
from jax.experimental.pallas import tpu as pltpu

HEAD_DIM = 128
GRID_W = 64
ROPE_THETA = 10000.0
LN_EPS = 1e-5
RMS_EPS = 1e-6
DEPTH = 4
DEEPNORM_ALPHA = (2 * DEPTH) ** 0.25
MLA_NOPE = 128
MLA_ROPE = 64
MLA_V = 128
NA_WIN_ROWS = 8
NA_WIN_COLS = 16
NA_QROWS = 4
NA_UROWS = NA_QROWS + NA_WIN_ROWS
GQA_KV_HEADS = 8
N_EXPERTS = 8
TOP_K = 2
MOE_GROUP = 256
MASK_VALUE = -1e30

V7X_VMEM_LIMIT_BYTES = 56 * 1024 * 1024
LANES = 128
FLASH_ROWS = 128
LOG2E = math.log2(math.e)

F32 = jnp.float32
BF16 = jnp.bfloat16


def _cparams(*sem):
    return pltpu.CompilerParams(dimension_semantics=sem, vmem_limit_bytes=V7X_VMEM_LIMIT_BYTES)


def _pick(n, pref):
    if n <= pref:
        return n
    t = pref
    while n % t:
        t //= 2
    return t


def _rot_half(xs, w, first):
    tw = xs.shape[1]
    if w == tw:
        return pltpu.roll(xs, w // 2, 1)
    return jnp.where(first, pltpu.roll(xs, tw - w // 2, 1), pltpu.roll(xs, w // 2, 1))


def _mm_kernel(*refs, expert, swiglu, rms_w, rope_w, scale, cast_b):
    refs = list(refs)
    ge_ref = refs.pop(0) if expert else None
    a_ref = refs.pop(0)
    b_ref = refs.pop(0)
    bu_ref = refs.pop(0) if swiglu else None
    gain_ref = refs.pop(0) if rms_w else None
    cos_ref = refs.pop(0) if rope_w else None
    sin_ref = refs.pop(0) if rope_w else None
    o_ref = refs.pop(0)
    i = pl.program_id(1)

    if cast_b:
        wb_ref = refs.pop(0)
        wu_ref = refs.pop(0) if swiglu else None
        changed = i == 0
        if expert:
            changed = changed | (ge_ref[i] != ge_ref[jnp.maximum(i - 1, 0)])

        @pl.when(changed)
        def _():
            wb_ref[...] = b_ref[...].astype(BF16)
            if swiglu:
                wu_ref[...] = bu_ref[...].astype(BF16)

        b_ref, bu_ref = wb_ref, wu_ref

    if expert:
        used = i < ge_ref[pl.num_programs(1)]

        @pl.when(jnp.logical_not(used))
        def _():
            o_ref[...] = jnp.zeros_like(o_ref)

        pl.when(used)(functools.partial(
            _mm_compute, a_ref, b_ref, bu_ref, gain_ref, cos_ref, sin_ref, o_ref,
            swiglu=swiglu, rms_w=rms_w, rope_w=rope_w, scale=scale))
    else:
        _mm_compute(a_ref, b_ref, bu_ref, gain_ref, cos_ref, sin_ref, o_ref,
                    swiglu=swiglu, rms_w=rms_w, rope_w=rope_w, scale=scale)


def _mm_compute(a_ref, b_ref, bu_ref, gain_ref, cos_ref, sin_ref, o_ref, *, swiglu, rms_w, rope_w, scale):
    a = a_ref[...]
    acc = jnp.dot(a, b_ref[...], preferred_element_type=F32)
    if swiglu:
        up = jnp.dot(a, bu_ref[...], preferred_element_type=F32)
        acc = acc * jax.nn.sigmoid(acc) * up
    tm, tn = acc.shape
    if rms_w:
        parts = []
        for c in range(tn // rms_w):
            xs = acc[:, c * rms_w:(c + 1) * rms_w]
            ms = jnp.mean(xs * xs, axis=-1, keepdims=True)
            parts.append(xs * lax.rsqrt(ms + RMS_EPS) * gain_ref[:, c * rms_w:(c + 1) * rms_w])
        acc = parts[0] if len(parts) == 1 else jnp.concatenate(parts, axis=1)
    if rope_w:
        tw = cos_ref.shape[1]
        cos = cos_ref[...]
        sin = sin_ref[...]
        lane = lax.broadcasted_iota(jnp.int32, (tm, tw), 1)
        first = (lane % rope_w) < (rope_w // 2)
        for c in range(tn // tw):
            xs = acc[:, c * tw:(c + 1) * tw]
            xs = xs * cos + _rot_half(xs, rope_w, first) * sin
            if scale != 1.0:
                xs = xs * scale
            o_ref[:, c * tw:(c + 1) * tw] = xs.astype(o_ref.dtype)
    else:
        if scale != 1.0:
            acc = acc * scale
        o_ref[...] = acc.astype(o_ref.dtype)


def _mm(a, b, *, n_out, col0=0, tm=1024, tn=512, out_dtype=BF16, swiglu=False,
        group_expert=None, rms_gain=None, rms_w=None, rope=None, rope_w=None, scale=1.0):
    m, k = a.shape
    tm = _pick(m, tm)
    tn = _pick(n_out, tn)
    assert m % tm == 0 and n_out % tn == 0 and col0 % tn == 0
    expert = group_expert is not None
    cb = col0 // tn
    ub = (b.shape[-1] // 2) // tn
    grid = (n_out // tn, m // tm)

    if expert:
        a_map = lambda j, i, ge: (i, 0)
        b_map = lambda j, i, ge: (ge[i], 0, cb + j)
        u_map = lambda j, i, ge: (ge[i], 0, cb + ub + j)
        c_map = lambda j, i, ge: (0, j)
        t_map = lambda j, i, ge: (i, 0)
        o_map = lambda j, i, ge: (i, j)
        b_block = (None, k, tn)
    else:
        a_map = lambda j, i: (i, 0)
        b_map = lambda j, i: (0, cb + j)
        u_map = lambda j, i: (0, cb + ub + j)
        c_map = lambda j, i: (0, j)
        t_map = lambda j, i: (i, 0)
        o_map = lambda j, i: (i, j)
        b_block = (k, tn)

    in_specs = [pl.BlockSpec((tm, k), a_map), pl.BlockSpec(b_block, b_map)]
    args = [a, b]
    if swiglu:
        assert b.shape[-1] % (2 * tn) == 0
        in_specs.append(pl.BlockSpec(b_block, u_map))
        args.append(b)
    if rms_w:
        assert tn % rms_w == 0
        in_specs.append(pl.BlockSpec((1, tn), c_map))
        args.append(rms_gain)
    if rope_w:
        cos, sin = rope
        tw = cos.shape[1]
        assert tn % tw == 0 and cos.shape[0] == m
        in_specs += [pl.BlockSpec((tm, tw), t_map), pl.BlockSpec((tm, tw), t_map)]
        args += [cos, sin]

    cast_b = b.dtype != BF16
    scratch = [pltpu.VMEM((k, tn), BF16)] * ((2 if swiglu else 1) if cast_b else 0)
    sem = ("parallel", "arbitrary") if cast_b else ("parallel", "parallel")
    kern = functools.partial(_mm_kernel, expert=expert, swiglu=swiglu, rms_w=rms_w,
                             rope_w=rope_w, scale=float(scale), cast_b=cast_b)
    out_shape = jax.ShapeDtypeStruct((m, n_out), out_dtype)
    out_spec = pl.BlockSpec((tm, tn), o_map)
    if expert:
        assert group_expert.shape[0] == m // tm + 1
        gs = pltpu.PrefetchScalarGridSpec(num_scalar_prefetch=1, grid=grid, in_specs=in_specs,
                                          out_specs=out_spec, scratch_shapes=scratch)
        return pl.pallas_call(kern, out_shape=out_shape, grid_spec=gs,
                              compiler_params=_cparams(*sem), name="expert_mm")(group_expert, *args)
    return pl.pallas_call(kern, out_shape=out_shape, grid=grid, in_specs=in_specs,
                          out_specs=out_spec, scratch_shapes=scratch,
                          compiler_params=_cparams(*sem), name="mm")(*args)


def _layer_norm_rows(z, g, b):
    mu = jnp.mean(z, axis=-1, keepdims=True)
    zc = z - mu
    var = jnp.mean(zc * zc, axis=-1, keepdims=True)
    return zc * lax.rsqrt(var + LN_EPS) * g + b


def _ln_kernel(x_ref, y_ref, g_ref, b_ref, o32_ref, o16_ref):
    z = DEEPNORM_ALPHA * x_ref[...] + y_ref[...].astype(F32)
    out = _layer_norm_rows(z, g_ref[...], b_ref[...])
    o32_ref[...] = out
    o16_ref[...] = out.astype(BF16)


def _residual_ln(x, y, g, b, *, tm=256):
    m, d = x.shape
    tm = _pick(m, tm)
    row = pl.BlockSpec((tm, d), lambda i: (i, 0))
    vec = pl.BlockSpec((1, d), lambda i: (0, 0))
    return pl.pallas_call(
        _ln_kernel,
        out_shape=(jax.ShapeDtypeStruct((m, d), F32), jax.ShapeDtypeStruct((m, d), BF16)),
        grid=(m // tm,), in_specs=[row, row, vec, vec], out_specs=(row, row),
        compiler_params=_cparams("parallel"), name="residual_ln",
    )(x, y, g.reshape(1, d), b.reshape(1, d))


def _flash_kernel(*refs, mode, n_rep, nk, lam_init):
    refs = list(refs)
    q_ref = refs.pop(0)
    k_ref = refs.pop(0)
    kr_ref = refs.pop(0) if mode == "mla" else None
    v_ref = refs.pop(0)
    if mode == "diff":
        lq1_ref, lk1_ref, lq2_ref, lk2_ref, sg_ref = [refs.pop(0) for _ in range(5)]
    o_ref = refs.pop(0)
    m_s, l_s, acc_s, s_s, p_s = refs

    kk = pl.program_id(3)

    @pl.when(kk == 0)
    def _():
        m_s[...] = jnp.full_like(m_s, -jnp.inf)
        l_s[...] = jnp.zeros_like(l_s)
        acc_s[...] = jnp.zeros_like(acc_s)

    q = q_ref[...]
    k = k_ref[...]
    if mode == "mla":
        k = jnp.concatenate([k, kr_ref[...]], axis=1)
    v = v_ref[...]
    tq, tk = s_s.shape[1], s_s.shape[2]
    dq = q.shape[1] // n_rep
    dv = acc_s.shape[2]
    for r in range(n_rep):
        q_r = q[:, r * dq:(r + 1) * dq]
        k_r = k[:, r * dq:(r + 1) * dq] if mode == "diff" else k
        s_s[r] = lax.dot_general(q_r, k_r, (((1,), (1,)), ((), ())), preferred_element_type=F32)
        for rb in range(tq // FLASH_ROWS):
            rows = slice(rb * FLASH_ROWS, (rb + 1) * FLASH_ROWS)
            mp = s_s[r, rows, 0:LANES]
            for c in range(1, tk // LANES):
                mp = jnp.maximum(mp, s_s[r, rows, c * LANES:(c + 1) * LANES])
            m_prev = m_s[r, rows, :]
            m_new = jnp.maximum(m_prev, jnp.max(mp, axis=-1, keepdims=True))
            alpha = jnp.exp2(m_prev - m_new)
            lp = jnp.zeros((FLASH_ROWS, LANES), F32)
            for c in range(tk // LANES):
                p = jnp.exp2(s_s[r, rows, c * LANES:(c + 1) * LANES] - m_new)
                lp = lp + p
                p_s[r, rows, c * LANES:(c + 1) * LANES] = p.astype(BF16)
            l_s[r, rows, :] = alpha * l_s[r, rows, :] + jnp.sum(lp, axis=-1, keepdims=True)
            m_s[r, rows, :] = m_new
            for c in range(dv // LANES):
                cols = slice(c * LANES, (c + 1) * LANES)
                acc_s[r, rows, cols] = acc_s[r, rows, cols] * alpha
        acc_s[r] += jnp.dot(p_s[r], v, preferred_element_type=F32)

    def normalised(r):
        acc, l = acc_s[r], l_s[r]
        parts = [acc[:, c * LANES:(c + 1) * LANES] / l for c in range(dv // LANES)]
        return parts[0] if len(parts) == 1 else jnp.concatenate(parts, axis=1)

    @pl.when(kk == nk - 1)
    def _():
        if mode == "diff":
            lam = (jnp.exp(jnp.sum(lq1_ref[...] * lk1_ref[...], axis=-1, keepdims=True))
                   - jnp.exp(jnp.sum(lq2_ref[...] * lk2_ref[...], axis=-1, keepdims=True)) + lam_init)
            o = normalised(0) - lam * normalised(1)
            ms = jnp.mean(o * o, axis=-1, keepdims=True)
            o = o * lax.rsqrt(ms + RMS_EPS) * sg_ref[...] * (1.0 - lam_init)
            o_ref[...] = o.astype(o_ref.dtype)
        else:
            for r in range(n_rep):
                o_ref[:, r * dv:(r + 1) * dv] = normalised(r).astype(o_ref.dtype)


def _flash(mode, q, k, v, *, row0, batch, seq, n_groups, kr=None, diff_params=None,
           lam_init=0.0, k_col=None, v_col=None, tq=512, tk=1024):
    tq = _pick(seq, tq)
    tk = _pick(seq, tk)
    nq, nk = seq // tq, seq // tk
    rq, rk = row0 // tq, row0 // tk
    assert row0 % tq == 0 and row0 % tk == 0
    if mode == "diff":
        n_rep, qw, kw, vw, ow = 2, 2 * HEAD_DIM, 2 * HEAD_DIM, 2 * HEAD_DIM, 2 * HEAD_DIM
    elif mode == "gqa":
        n_rep = q.shape[1] // (n_groups * HEAD_DIM)
        qw, kw, vw, ow = n_rep * HEAD_DIM, HEAD_DIM, HEAD_DIM, n_rep * HEAD_DIM
    else:
        n_rep, qw, kw, vw, ow = 1, 2 * HEAD_DIM, HEAD_DIM, HEAD_DIM, HEAD_DIM
    k_col = k_col or (lambda h: h)
    v_col = v_col or (lambda h: h)

    in_specs = [
        pl.BlockSpec((tq, qw), lambda b, h, i, kk: (rq + b * nq + i, h)),
        pl.BlockSpec((tk, kw), lambda b, h, i, kk: (rk + b * nk + kk, k_col(h))),
    ]
    args = [q, k]
    if mode == "mla":
        in_specs.append(pl.BlockSpec((tk, HEAD_DIM), lambda b, h, i, kk: (rk + b * nk + kk, 0)))
        args.append(kr)
    in_specs.append(pl.BlockSpec((tk, vw), lambda b, h, i, kk: (rk + b * nk + kk, v_col(h))))
    args.append(v)
    if mode == "diff":
        for p in diff_params:
            in_specs.append(pl.BlockSpec((1, p.shape[1]), lambda b, h, i, kk: (0, 0)))
            args.append(p)
    dv = vw
    kern = functools.partial(_flash_kernel, mode=mode, n_rep=n_rep, nk=nk, lam_init=lam_init)
    return pl.pallas_call(
        kern,
        out_shape=jax.ShapeDtypeStruct((batch * seq, n_groups * ow), BF16),
        grid=(batch, n_groups, nq, nk),
        in_specs=in_specs,
        out_specs=pl.BlockSpec((tq, ow), lambda b, h, i, kk: (b * nq + i, h)),
        scratch_shapes=[pltpu.VMEM((n_rep, tq, LANES), F32), pltpu.VMEM((n_rep, tq, LANES), F32),
                        pltpu.VMEM((n_rep, tq, dv), F32),
                        pltpu.VMEM((n_rep, tq, tk), F32), pltpu.VMEM((n_rep, tq, tk), BF16)],
        compiler_params=_cparams("parallel", "parallel", "parallel", "arbitrary"),
        name="flash_" + mode,
    )(*args)


def _na_kernel(q_ref, k_ref, v_ref, bias_ref, o_ref, *, n_rows, heads, scale):
    r0 = pl.program_id(2) * NA_QROWS
    us = jnp.clip(r0 - NA_WIN_ROWS // 2, 0, n_rows - NA_UROWS)
    start = pl.multiple_of(us * GRID_W, GRID_W)
    kb = k_ref[pl.ds(start, NA_UROWS * GRID_W), :]
    vb = v_ref[pl.ds(start, NA_UROWS * GRID_W), :]
    for h in range(heads):
        sl = slice(h * HEAD_DIM, (h + 1) * HEAD_DIM)
        s = lax.dot_general(q_ref[:, sl], kb[:, sl], (((1,), (1,)), ((), ())),
                            preferred_element_type=F32)
        s = s * scale + bias_ref[h]
        m = jnp.max(s, axis=-1, keepdims=True)
        p = jnp.exp(s - m)
        l = jnp.sum(p, axis=-1, keepdims=True)
        o = jnp.dot(p.astype(BF16), vb[:, sl], preferred_element_type=F32) / l
        o_ref[:, sl] = o.astype(o_ref.dtype)


def _na_bias_table(rpb):
    c = np.arange(GRID_W)
    cs = np.clip(c - NA_WIN_COLS // 2, 0, GRID_W - NA_WIN_COLS)
    col_in = (c[None, :] >= cs[:, None]) & (c[None, :] < cs[:, None] + NA_WIN_COLS)
    dc_idx = np.clip(c[None, :] - c[:, None], -(NA_WIN_COLS - 1), NA_WIN_COLS - 1) + NA_WIN_COLS - 1
    t = np.arange(NA_QROWS)[:, None]
    j = np.arange(NA_UROWS)[None, :]
    half = NA_WIN_ROWS // 2
    own_start = np.stack([0 * t, t, 0 * t + (NA_UROWS - NA_WIN_ROWS)])
    q_row = np.stack([t, t + half, t + (NA_UROWS - NA_QROWS)])
    row_in = (j[None] >= own_start) & (j[None] < own_start + NA_WIN_ROWS)
    dr_idx = np.clip(j[None] - q_row + NA_WIN_ROWS - 1, 0, 2 * NA_WIN_ROWS - 2)
    rows = rpb.astype(F32)[:, dr_idx, :]
    onehot = (dc_idx[:, :, None] == np.arange(2 * NA_WIN_COLS - 1)).astype(np.float32)
    tab = jnp.einsum("hctjd,qkd->hctqjk", rows, onehot, precision=lax.Precision.HIGHEST)
    keep = row_in[None, :, :, None, :, None] & col_in[None, None, None, :, None, :]
    tab = jnp.where(keep, tab, MASK_VALUE)
    return tab.reshape(rpb.shape[0], 3, NA_QROWS * GRID_W, NA_UROWS * GRID_W)


def _neighbourhood(qkv, bias, *, row0, batch, seq, n_heads, heads_per_step=4):
    n_rows = seq // GRID_W
    assert n_rows >= NA_UROWS and n_rows % NA_QROWS == 0
    hb = heads_per_step
    nhb = n_heads // hb
    n_steps = n_rows // NA_QROWS
    qb = NA_QROWS * GRID_W
    r0 = row0 // qb
    b0 = row0 // seq
    assert row0 % seq == 0

    def bias_map(b, h, i):
        return (h, jnp.where(i == 0, 0, jnp.where(i == n_steps - 1, 2, 1)), 0, 0)

    kern = functools.partial(_na_kernel, n_rows=n_rows, heads=hb, scale=HEAD_DIM ** -0.5)
    return pl.pallas_call(
        kern,
        out_shape=jax.ShapeDtypeStruct((batch * seq, n_heads * HEAD_DIM), BF16),
        grid=(batch, nhb, n_steps),
        in_specs=[
            pl.BlockSpec((qb, hb * HEAD_DIM), lambda b, h, i: (r0 + b * n_steps + i, h)),
            pl.BlockSpec((seq, hb * HEAD_DIM), lambda b, h, i: (b0 + b, nhb + h)),
            pl.BlockSpec((seq, hb * HEAD_DIM), lambda b, h, i: (b0 + b, 2 * nhb + h)),
            pl.BlockSpec((hb, None, qb, NA_UROWS * GRID_W), bias_map),
        ],
        out_specs=pl.BlockSpec((qb, hb * HEAD_DIM), lambda b, h, i: (b * n_steps + i, h)),
        compiler_params=_cparams("parallel", "parallel", "arbitrary"),
        name="neighbourhood_attention",
    )(qkv, qkv, qkv, bias)


def _router_kernel(x_ref, w_ref, o_ref):
    logits = jnp.dot(x_ref[...], w_ref[...], preferred_element_type=F32,
                     precision=lax.Precision.HIGHEST)
    lane = lax.broadcasted_iota(jnp.int32, logits.shape, 1)
    logits = jnp.where(lane < N_EXPERTS, logits, -jnp.inf)
    m1 = jnp.max(logits, axis=-1, keepdims=True)
    i1 = jnp.min(jnp.where(logits == m1, lane, LANES), axis=-1, keepdims=True)
    rest = jnp.where(lane == i1, -jnp.inf, logits)
    m2 = jnp.max(rest, axis=-1, keepdims=True)
    i2 = jnp.min(jnp.where(rest == m2, lane, LANES), axis=-1, keepdims=True)
    e2 = jnp.exp(m2 - m1)
    g1 = 1.0 / (1.0 + e2)
    g2 = e2 / (1.0 + e2)
    out = jnp.where(lane == 0, i1.astype(F32), 0.0)
    out = jnp.where(lane == 1, i2.astype(F32), out)
    out = jnp.where(lane == 2, g1, out)
    out = jnp.where(lane == 3, g2, out)
    o_ref[...] = out


def _router(x, w_router, *, tm=512):
    m, d = x.shape
    tm = _pick(m, tm)
    w = jnp.zeros((d, LANES), F32).at[:, :N_EXPERTS].set(w_router.astype(F32))
    return pl.pallas_call(
        _router_kernel,
        out_shape=jax.ShapeDtypeStruct((m, LANES), F32),
        grid=(m // tm,),
        in_specs=[pl.BlockSpec((tm, d), lambda i: (i, 0)), pl.BlockSpec((d, LANES), lambda i: (0, 0))],
        out_specs=pl.BlockSpec((tm, LANES), lambda i: (i, 0)),
        compiler_params=_cparams("parallel"), name="router",
    )(x, w)


def _row_copy(src_hbm, row, dst, dst_row, sem):
    return pltpu.make_async_copy(src_hbm.at[pl.ds(row, 1)], dst.at[pl.ds(dst_row, 1)], sem)


def _gather_kernel(tok_ref, x_hbm, o_ref, buf, sem, *, rows):
    g = pl.program_id(0)
    slot = g % 2

    def fetch(step, into):
        def issue(t, c):
            _row_copy(x_hbm, tok_ref[step * rows + t], buf.at[into], t, sem.at[into]).start()
            return c
        lax.fori_loop(0, rows, issue, 0)

    @pl.when(g == 0)
    def _():
        fetch(0, 0)

    @pl.when(g + 1 < pl.num_programs(0))
    def _():
        fetch(g + 1, 1 - slot)

    def wait(t, c):
        _row_copy(x_hbm, 0, buf.at[slot], t, sem.at[slot]).wait()
        return c

    lax.fori_loop(0, rows, wait, 0)
    o_ref[...] = buf[slot].astype(o_ref.dtype)


def _gather_rows(x, slot_tok, *, rows=MOE_GROUP):
    n_slots = slot_tok.shape[0]
    d = x.shape[1]
    gs = pltpu.PrefetchScalarGridSpec(
        num_scalar_prefetch=1, grid=(n_slots // rows,),
        in_specs=[pl.BlockSpec(memory_space=pl.ANY)],
        out_specs=pl.BlockSpec((rows, d), lambda g, tok: (g, 0)),
        scratch_shapes=[pltpu.VMEM((2, rows, d), x.dtype), pltpu.SemaphoreType.DMA((2,))],
    )
    return pl.pallas_call(
        functools.partial(_gather_kernel, rows=rows),
        out_shape=jax.ShapeDtypeStruct((n_slots, d), BF16), grid_spec=gs,
        compiler_params=_cparams("arbitrary"), name="moe_gather",
    )(slot_tok, x)


def _combine_kernel(pos_ref, ys_hbm, x_ref, gate_ref, g_ref, b_ref, o32_ref, o16_ref, buf, sem, *, rows):
    i = pl.program_id(0)
    slot = i % 2

    def fetch(step, into):
        def issue(t, c):
            for kx in range(TOP_K):
                _row_copy(ys_hbm, pos_ref[(step * rows + t) * TOP_K + kx], buf.at[into, kx], t,
                          sem.at[into]).start()
            return c
        lax.fori_loop(0, rows, issue, 0)

    @pl.when(i == 0)
    def _():
        fetch(0, 0)

    @pl.when(i + 1 < pl.num_programs(0))
    def _():
        fetch(i + 1, 1 - slot)

    def wait(t, c):
        for kx in range(TOP_K):
            _row_copy(ys_hbm, 0, buf.at[slot, kx], t, sem.at[slot]).wait()
        return c

    lax.fori_loop(0, rows, wait, 0)
    gate = gate_ref[...]
    moe = buf[slot, 0] * gate[:, 0:1] + buf[slot, 1] * gate[:, 1:2]
    out = _layer_norm_rows(DEEPNORM_ALPHA * x_ref[...] + moe, g_ref[...], b_ref[...])
    o32_ref[...] = out
    o16_ref[...] = out.astype(BF16)


def _combine_ln(ys, pos, gates, x, g, b, *, rows=128):
    m, d = x.shape
    rows = _pick(m, rows)
    row = lambda i, pos: (i, 0)
    gs = pltpu.PrefetchScalarGridSpec(
        num_scalar_prefetch=1, grid=(m // rows,),
        in_specs=[pl.BlockSpec(memory_space=pl.ANY),
                  pl.BlockSpec((rows, d), row),
                  pl.BlockSpec((rows, TOP_K), row),
                  pl.BlockSpec((1, d), lambda i, pos: (0, 0)),
                  pl.BlockSpec((1, d), lambda i, pos: (0, 0))],
        out_specs=(pl.BlockSpec((rows, d), row), pl.BlockSpec((rows, d), row)),
        scratch_shapes=[pltpu.VMEM((2, TOP_K, rows, d), ys.dtype), pltpu.SemaphoreType.DMA((2,))],
    )
    return pl.pallas_call(
        functools.partial(_combine_kernel, rows=rows),
        out_shape=(jax.ShapeDtypeStruct((m, d), F32), jax.ShapeDtypeStruct((m, d), BF16)),
        grid_spec=gs, compiler_params=_cparams("arbitrary"), name="moe_combine_ln",
    )(pos.reshape(-1), ys, x, gates, g.reshape(1, d), b.reshape(1, d))


def _moe_layer(x32, x16, w_router, w_gate_up, w_down, ln_g, ln_b):
    n_tok, d = x32.shape
    fe = w_down.shape[1]
    route = _router(x32, w_router)
    e_idx = route[:, :TOP_K].astype(jnp.int32)
    gates = route[:, TOP_K:2 * TOP_K]
    flat_e = e_idx.reshape(-1)
    n_asg = flat_e.shape[0]
    onehot = (flat_e[:, None] == jnp.arange(N_EXPERTS, dtype=jnp.int32)[None, :]).astype(jnp.int32)
    csum = jnp.cumsum(onehot, axis=0)
    rank = jnp.sum((csum - onehot) * onehot, axis=1)
    counts = csum[-1]
    padded = (counts + MOE_GROUP - 1) // MOE_GROUP * MOE_GROUP
    padded_end = jnp.cumsum(padded)
    pos = (padded_end - padded)[flat_e] + rank
    n_slots = -(-n_asg // MOE_GROUP) * MOE_GROUP + N_EXPERTS * MOE_GROUP
    n_groups = n_slots // MOE_GROUP
    flat_tok = jnp.repeat(jnp.arange(n_tok, dtype=jnp.int32), TOP_K)
    slot_tok = jnp.zeros((n_slots,), jnp.int32).at[pos].set(flat_tok, unique_indices=True)
    group_start = jnp.arange(n_groups, dtype=jnp.int32) * MOE_GROUP
    group_expert = jnp.minimum(jnp.sum(padded_end[None, :] <= group_start[:, None], axis=1),
                               N_EXPERTS - 1).astype(jnp.int32)
    n_used = (padded_end[-1] // MOE_GROUP).astype(jnp.int32)
    group_expert = jnp.concatenate([group_expert, n_used[None]])

    xs = _gather_rows(x32, slot_tok)
    hs = _mm(xs, w_gate_up, n_out=fe, tm=MOE_GROUP, tn=512, swiglu=True, group_expert=group_expert)
    ys = _mm(hs, w_down, n_out=d, tm=MOE_GROUP, tn=512, out_dtype=F32, group_expert=group_expert)
    return _combine_ln(ys, pos.reshape(n_tok, TOP_K), gates, x32, ln_g, ln_b)


def _rope_tables(pos, dim):
    inv = ROPE_THETA ** (-jnp.arange(0, dim, 2, dtype=F32) / dim)
    ang = pos[:, None] * inv[None, :]
    ang = jnp.concatenate([ang, ang], axis=-1)
    sign = jnp.concatenate([-jnp.ones((dim // 2,), F32), jnp.ones((dim // 2,), F32)])
    return jnp.cos(ang), jnp.sin(ang) * sign[None, :]


def _lambda_init(layer_idx):
    return 0.8 - 0.6 * math.exp(-0.3 * layer_idx)


def kernel(x_prompt, x_sample, l0_w_qkv, l0_lambda_q1, l0_lambda_k1, l0_lambda_q2, l0_lambda_k2, l0_subln_g, l0_w_o, l0_ln1_g, l0_ln1_b, l0_ffn_w_gate_up, l0_ffn_w_down, l0_ln2_g, l0_ln2_b, l1_w_dqkv, l1_q_norm_g, l1_w_uq, l1_kv_norm_g, l1_w_ukv, l1_w_o, l1_ln1_g, l1_ln1_b, l1_router, l1_exp_w_gate_up, l1_exp_w_down, l1_ln2_g, l1_ln2_b, l2_w_qkv, l2_rpb, l2_w_o, l2_ln1_g, l2_ln1_b, l2_ffn_w_gate_up, l2_ffn_w_down, l2_ln2_g, l2_ln2_b, l3_w_qkv, l3_q_norm_g, l3_k_norm_g, l3_w_o, l3_ln1_g, l3_ln1_b, l3_router, l3_exp_w_gate_up, l3_exp_w_down, l3_ln2_g, l3_ln2_b):
    d = x_prompt.shape[-1]
    streams = ((x_prompt.shape[0], x_prompt.shape[1]), (x_sample.shape[0], x_sample.shape[1]))
    n_p = streams[0][0] * streams[0][1]
    n_s = streams[1][0] * streams[1][1]
    row0s = (0, n_p)
    n_heads = d // HEAD_DIM

    x32 = jnp.concatenate([x_prompt.reshape(n_p, d), x_sample.reshape(n_s, d)], axis=0)
    x16 = x32.astype(BF16)
    pos = jnp.concatenate([jnp.tile(jnp.arange(s, dtype=jnp.int32), b) for b, s in streams])
    posf = pos.astype(F32)

    def per_stream(fn):
        return jnp.concatenate([fn(row0, b, s) for row0, (b, s) in zip(row0s, streams)], axis=0)

    def dense_ffn(x32, x16, w_gate_up, w_down, g, b):
        f = w_down.shape[0]
        h = _mm(x16, w_gate_up.astype(BF16), n_out=f, tm=1024, tn=512, swiglu=True)
        y = _mm(h, w_down.astype(BF16), n_out=d, tm=256, tn=512, out_dtype=F32)
        return _residual_ln(x32, y, g, b)

    def attn_out(x32, o, w_o, g, b):
        y = _mm(o, w_o.astype(BF16), n_out=d, out_dtype=F32)
        return _residual_ln(x32, y, g, b)

    w = l0_w_qkv.astype(BF16)
    rope128 = _rope_tables(posf, HEAD_DIM)
    q = _mm(x16, w, n_out=d, col0=0, rope=rope128, rope_w=HEAD_DIM, scale=HEAD_DIM ** -0.5 * LOG2E)
    k = _mm(x16, w, n_out=d, col0=d, rope=rope128, rope_w=HEAD_DIM)
    v = _mm(x16, w, n_out=d, col0=2 * d)
    lam0 = _lambda_init(0)
    diff_params = [p.astype(F32).reshape(1, -1) for p in
                   (l0_lambda_q1, l0_lambda_k1, l0_lambda_q2, l0_lambda_k2, l0_subln_g)]
    o = per_stream(lambda row0, b, s: _flash(
        "diff", q, k, v, row0=row0, batch=b, seq=s, n_groups=n_heads // 2,
        diff_params=diff_params, lam_init=lam0, tq=1024, tk=2048))
    x32, x16 = attn_out(x32, o, l0_w_o, l0_ln1_g, l0_ln1_b)
    x32, x16 = dense_ffn(x32, x16, l0_ffn_w_gate_up, l0_ffn_w_down, l0_ln2_g, l0_ln2_b)

    q_lora = l1_q_norm_g.shape[0]
    kv_lora = l1_kv_norm_g.shape[0]
    w = l1_w_dqkv.astype(BF16)
    c_q = _mm(x16, w, n_out=q_lora, col0=0, tn=q_lora,
              rms_gain=l1_q_norm_g.reshape(1, -1), rms_w=q_lora)
    c_kv = _mm(x16, w, n_out=kv_lora, col0=q_lora, tn=kv_lora,
               rms_gain=l1_kv_norm_g.reshape(1, -1), rms_w=kv_lora)
    cos64, sin64 = _rope_tables(posf, MLA_ROPE)
    n_tok = x32.shape[0]
    pad1 = jnp.ones((n_tok, HEAD_DIM - MLA_ROPE), F32)
    pad0 = jnp.zeros((n_tok, HEAD_DIM - MLA_ROPE), F32)
    kr_tab = (jnp.concatenate([cos64, pad1], axis=1), jnp.concatenate([sin64, pad0], axis=1))
    q_tab = (jnp.concatenate([jnp.ones((n_tok, MLA_NOPE), F32), kr_tab[0]], axis=1),
             jnp.concatenate([jnp.zeros((n_tok, MLA_NOPE), F32), kr_tab[1]], axis=1))
    w_kr = jnp.pad(w[:, q_lora + kv_lora:], ((0, 0), (0, HEAD_DIM - MLA_ROPE)))
    k_rope = _mm(x16, w_kr, n_out=HEAD_DIM, tn=HEAD_DIM, rope=kr_tab, rope_w=MLA_ROPE)
    w_uq = l1_w_uq.astype(BF16).reshape(q_lora, n_heads, MLA_NOPE + MLA_ROPE)
    w_uq = jnp.pad(w_uq, ((0, 0), (0, 0), (0, 2 * HEAD_DIM - MLA_NOPE - MLA_ROPE)))
    w_uq = w_uq.reshape(q_lora, n_heads * 2 * HEAD_DIM)
    q = _mm(c_q, w_uq, n_out=n_heads * 2 * HEAD_DIM, rope=q_tab, rope_w=MLA_ROPE,
            scale=(MLA_NOPE + MLA_ROPE) ** -0.5 * LOG2E)
    kv = _mm(c_kv, l1_w_ukv.astype(BF16), n_out=n_heads * (MLA_NOPE + MLA_V))
    o = per_stream(lambda row0, b, s: _flash(
        "mla", q, kv, kv, kr=k_rope, row0=row0, batch=b, seq=s, n_groups=n_heads,
        k_col=lambda h: 2 * h, v_col=lambda h: 2 * h + 1, tq=2048, tk=1024))
    x32, x16 = attn_out(x32, o, l1_w_o, l1_ln1_g, l1_ln1_b)
    x32, x16 = _moe_layer(x32, x16, l1_router, l1_exp_w_gate_up.astype(F32),
                          l1_exp_w_down.astype(BF16), l1_ln2_g, l1_ln2_b)

    qkv = _mm(x16, l2_w_qkv.astype(BF16), n_out=3 * d)
    bias = _na_bias_table(l2_rpb)
    o = per_stream(lambda row0, b, s: _neighbourhood(
        qkv, bias, row0=row0, batch=b, seq=s, n_heads=n_heads))
    x32, x16 = attn_out(x32, o, l2_w_o, l2_ln1_g, l2_ln1_b)
    x32, x16 = dense_ffn(x32, x16, l2_ffn_w_gate_up, l2_ffn_w_down, l2_ln2_g, l2_ln2_b)

    w = l3_w_qkv.astype(BF16)
    kv_w = GQA_KV_HEADS * HEAD_DIM
    half = HEAD_DIM // 2
    cr, sr = _rope_tables((pos // GRID_W).astype(F32), half)
    cc, sc = _rope_tables((pos % GRID_W).astype(F32), half)
    axial = (jnp.concatenate([cr, cc], axis=1), jnp.concatenate([sr, sc], axis=1))
    qg = jnp.tile(l3_q_norm_g.astype(F32), n_heads).reshape(1, -1)
    kg = jnp.tile(l3_k_norm_g.astype(F32), GQA_KV_HEADS).reshape(1, -1)
    q = _mm(x16, w, n_out=d, col0=0, rms_gain=qg, rms_w=HEAD_DIM, rope=axial, rope_w=half,
            scale=HEAD_DIM ** -0.5 * LOG2E)
    k = _mm(x16, w, n_out=kv_w, col0=d, rms_gain=kg, rms_w=HEAD_DIM, rope=axial, rope_w=half)
    v = _mm(x16, w, n_out=kv_w, col0=d + kv_w)
    o = per_stream(lambda row0, b, s: _flash(
        "gqa", q, k, v, row0=row0, batch=b, seq=s, n_groups=GQA_KV_HEADS, tq=512, tk=2048))
    x32, x16 = attn_out(x32, o, l3_w_o, l3_ln1_g, l3_ln1_b)
    x32, x16 = _moe_layer(x32, x16, l3_router, l3_exp_w_gate_up.astype(F32),
                          l3_exp_w_down.astype(BF16), l3_ln2_g, l3_ln2_b)

    return (x32[:n_p].reshape(x_prompt.shape), x32[n_p:].reshape(x_sample.shape))
```

```python
import functools
import math

import numpy as np
import jax
import jax.numpy as jnp
from jax import lax
from jax.experimental import pallas as pl
from jax.experimental.pallas import tpu as pltpu

HEAD_DIM = 128
GRID_W = 64
ROPE_THETA = 10000.0
LN_EPS = 1e-5
RMS_EPS = 1e-6
DEPTH = 4
DEEPNORM_ALPHA = (2 * DEPTH) ** 0.25
MLA_NOPE = 128
MLA_ROPE = 64
MLA_V = 128
NA_WIN_ROWS = 8
NA_WIN_COLS = 16
NA_QROWS = 4
NA_UROWS = NA_QROWS + NA_WIN_ROWS
GQA_KV_HEADS = 8
N_EXPERTS = 8
TOP_K = 2
MOE_GROUP = 256
MASK_VALUE = -1e30

V7X_VMEM_LIMIT_BYTES = 56 * 1024 * 1024
LANES = 128
FLASH_ROWS = 128
LOG2E = math.log2(math.e)

F32 = jnp.float32
BF16 = jnp.bfloat16


def _cparams(*sem):
    return pltpu.CompilerParams(dimension_semantics=sem, vmem_limit_bytes=V7X_VMEM_LIMIT_BYTES)


def _pick(n, pref):
    if n <= pref:
        return n
    t = pref
    while n % t:
        t //= 2
    return t


def _rot_half(xs, w, first):
    tw = xs.shape[1]
    if w == tw:
        return pltpu.roll(xs, w // 2, 1)
    return jnp.where(first, pltpu.roll(xs, tw - w // 2, 1), pltpu.roll(xs, w // 2, 1))


def _mm_kernel(*refs, expert, swiglu, rms_w, rope_w, scale):
    refs = list(refs)
    ge_ref = refs.pop(0) if expert else None
    a_ref = refs.pop(0)
    b_ref = refs.pop(0)
    bu_ref = refs.pop(0) if swiglu else None
    gain_ref = refs.pop(0) if rms_w else None
    cos_ref = refs.pop(0) if rope_w else None
    sin_ref = refs.pop(0) if rope_w else None
    o_ref = refs.pop(0)
    i = pl.program_id(1)

    if expert:
        used = i < ge_ref[pl.num_programs(1)]

        @pl.when(jnp.logical_not(used))
        def _():
            o_ref[...] = jnp.zeros_like(o_ref)

        pl.when(used)(functools.partial(
            _mm_compute, a_ref, b_ref, bu_ref, gain_ref, cos_ref, sin_ref, o_ref,
            swiglu=swiglu, rms_w=rms_w, rope_w=rope_w, scale=scale))
    else:
        _mm_compute(a_ref, b_ref, bu_ref, gain_ref, cos_ref, sin_ref, o_ref,
                    swiglu=swiglu, rms_w=rms_w, rope_w=rope_w, scale=scale)


def _mm_compute(a_ref, b_ref, bu_ref, gain_ref, cos_ref, sin_ref, o_ref, *, swiglu, rms_w, rope_w, scale):
    a = a_ref[...]
    acc = jnp.dot(a, b_ref[...].astype(BF16), preferred_element_type=F32)
    if swiglu:
        up = jnp.dot(a, bu_ref[...].astype(BF16), preferred_element_type=F32)
        acc = acc * jax.nn.sigmoid(acc) * up
    tm, tn = acc.shape
    if rms_w:
        parts = []
        for c in range(tn // rms_w):
            xs = acc[:, c * rms_w:(c + 1) * rms_w]
            ms = jnp.mean(xs * xs, axis=-1, keepdims=True)
            parts.append(xs * lax.rsqrt(ms + RMS_EPS) * gain_ref[:, c * rms_w:(c + 1) * rms_w])
        acc = parts[0] if len(parts) == 1 else jnp.concatenate(parts, axis=1)
    if rope_w:
        tw = cos_ref.shape[1]
        cos = cos_ref[...]
        sin = sin_ref[...]
        lane = lax.broadcasted_iota(jnp.int32, (tm, tw), 1)
        first = (lane % rope_w) < (rope_w // 2)
        for c in range(tn // tw):
            xs = acc[:, c * tw:(c + 1) * tw]
            xs = xs * cos + _rot_half(xs, rope_w, first) * sin
            if scale != 1.0:
                xs = xs * scale
            o_ref[:, c * tw:(c + 1) * tw] = xs.astype(o_ref.dtype)
    else:
        if scale != 1.0:
            acc = acc * scale
        o_ref[...] = acc.astype(o_ref.dtype)


def _mm(a, b, *, n_out, col0=0, tm=1024, tn=512, out_dtype=BF16, swiglu=False,
        group_expert=None, rms_gain=None, rms_w=None, rope=None, rope_w=None, scale=1.0):
    m, k = a.shape
    tm = _pick(m, tm)
    tn = _pick(n_out, tn)
    assert m % tm == 0 and n_out % tn == 0 and col0 % tn == 0
    expert = group_expert is not None
    cb = col0 // tn
    ub = (b.shape[-1] // 2) // tn
    grid = (n_out // tn, m // tm)

    if expert:
        a_map = lambda j, i, ge: (i, 0)
        b_map = lambda j, i, ge: (ge[i], 0, cb + j)
        u_map = lambda j, i, ge: (ge[i], 0, cb + ub + j)
        c_map = lambda j, i, ge: (0, j)
        t_map = lambda j, i, ge: (i, 0)
        o_map = lambda j, i, ge: (i, j)
        b_block = (None, k, tn)
    else:
        a_map = lambda j, i: (i, 0)
        b_map = lambda j, i: (0, cb + j)
        u_map = lambda j, i: (0, cb + ub + j)
        c_map = lambda j, i: (0, j)
        t_map = lambda j, i: (i, 0)
        o_map = lambda j, i: (i, j)
        b_block = (k, tn)

    in_specs = [pl.BlockSpec((tm, k), a_map), pl.BlockSpec(b_block, b_map)]
    args = [a, b]
    if swiglu:
        assert b.shape[-1] % (2 * tn) == 0
        in_specs.append(pl.BlockSpec(b_block, u_map))
        args.append(b)
    if rms_w:
        assert tn % rms_w == 0
        in_specs.append(pl.BlockSpec((1, tn), c_map))
        args.append(rms_gain)
    if rope_w:
        cos, sin = rope
        tw = cos.shape[1]
        assert tn % tw == 0 and cos.shape[0] == m
        in_specs += [pl.BlockSpec((tm, tw), t_map), pl.BlockSpec((tm, tw), t_map)]
        args += [cos, sin]

    scratch = []
    sem = ("parallel", "parallel")
    kern = functools.partial(_mm_kernel, expert=expert, swiglu=swiglu, rms_w=rms_w,
                             rope_w=rope_w, scale=float(scale))
    out_shape = jax.ShapeDtypeStruct((m, n_out), out_dtype)
    out_spec = pl.BlockSpec((tm, tn), o_map)
    if expert:
        assert group_expert.shape[0] == m // tm + 1
        gs = pltpu.PrefetchScalarGridSpec(num_scalar_prefetch=1, grid=grid, in_specs=in_specs,
                                          out_specs=out_spec, scratch_shapes=scratch)
        return pl.pallas_call(kern, out_shape=out_shape, grid_spec=gs,
                              compiler_params=_cparams(*sem), name="expert_mm")(group_expert, *args)
    return pl.pallas_call(kern, out_shape=out_shape, grid=grid, in_specs=in_specs,
                          out_specs=out_spec, scratch_shapes=scratch,
                          compiler_params=_cparams(*sem), name="mm")(*args)


def _layer_norm_rows(z, g, b):
    mu = jnp.mean(z, axis=-1, keepdims=True)
    zc = z - mu
    var = jnp.mean(zc * zc, axis=-1, keepdims=True)
    return zc * lax.rsqrt(var + LN_EPS) * g + b


def _ln_kernel(x_ref, y_ref, g_ref, b_ref, o32_ref, o16_ref):
    z = DEEPNORM_ALPHA * x_ref[...] + y_ref[...].astype(F32)
    out = _layer_norm_rows(z, g_ref[...], b_ref[...])
    o32_ref[...] = out
    o16_ref[...] = out.astype(BF16)


def _residual_ln(x, y, g, b, *, tm=256):
    m, d = x.shape
    tm = _pick(m, tm)
    row = pl.BlockSpec((tm, d), lambda i: (i, 0))
    vec = pl.BlockSpec((1, d), lambda i: (0, 0))
    return pl.pallas_call(
        _ln_kernel,
        out_shape=(jax.ShapeDtypeStruct((m, d), F32), jax.ShapeDtypeStruct((m, d), BF16)),
        grid=(m // tm,), in_specs=[row, row, vec, vec], out_specs=(row, row),
        compiler_params=_cparams("parallel"), name="residual_ln",
    )(x, y, g.reshape(1, d), b.reshape(1, d))


def _flash_kernel(*refs, mode, n_rep, nk, lam_init):
    refs = list(refs)
    q_ref = refs.pop(0)
    k_ref = refs.pop(0)
    kr_ref = refs.pop(0) if mode == "mla" else None
    v_ref = refs.pop(0)
    if mode == "diff":
        lq1_ref, lk1_ref, lq2_ref, lk2_ref, sg_ref = [refs.pop(0) for _ in range(5)]
    o_ref = refs.pop(0)
    m_s, l_s, acc_s, s_s, p_s = refs

    kk = pl.program_id(3)

    @pl.when(kk == 0)
    def _():
        m_s[...] = jnp.full_like(m_s, -jnp.inf)
        l_s[...] = jnp.zeros_like(l_s)
        acc_s[...] = jnp.zeros_like(acc_s)

    q = q_ref[...]
    k = k_ref[...]
    if mode == "mla":
        k = jnp.concatenate([k, kr_ref[...]], axis=1)
    v = v_ref[...]
    tq, tk = s_s.shape[1], s_s.shape[2]
    dq = q.shape[1] // n_rep
    dv = acc_s.shape[2]
    for r in range(n_rep):
        q_r = q[:, r * dq:(r + 1) * dq]
        k_r = k[:, r * dq:(r + 1) * dq] if mode == "diff" else k
        s_s[r] = lax.dot_general(q_r, k_r, (((1,), (1,)), ((), ())), preferred_element_type=F32)
        for rb in range(tq // FLASH_ROWS):
            rows = slice(rb * FLASH_ROWS, (rb + 1) * FLASH_ROWS)
            mp = s_s[r, rows, 0:LANES]
            for c in range(1, tk // LANES):
                mp = jnp.maximum(mp, s_s[r, rows, c * LANES:(c + 1) * LANES])
            m_prev = m_s[r, rows, :]
            m_new = jnp.maximum(m_prev, jnp.max(mp, axis=-1, keepdims=True))
            alpha = jnp.exp2(m_prev - m_new)
            lp = jnp.zeros((FLASH_ROWS, LANES), F32)
            for c in range(tk // LANES):
                p = jnp.exp2(s_s[r, rows, c * LANES:(c + 1) * LANES] - m_new)
                lp = lp + p
                p_s[r, rows, c * LANES:(c + 1) * LANES] = p.astype(BF16)
            l_s[r, rows, :] = alpha * l_s[r, rows, :] + jnp.sum(lp, axis=-1, keepdims=True)
            m_s[r, rows, :] = m_new
            for c in range(dv // LANES):
                cols = slice(c * LANES, (c + 1) * LANES)
                acc_s[r, rows, cols] = acc_s[r, rows, cols] * alpha
        acc_s[r] += jnp.dot(p_s[r], v, preferred_element_type=F32)

    def normalised(r):
        acc, l = acc_s[r], l_s[r]
        parts = [acc[:, c * LANES:(c + 1) * LANES] / l for c in range(dv // LANES)]
        return parts[0] if len(parts) == 1 else jnp.concatenate(parts, axis=1)

    @pl.when(kk == nk - 1)
    def _():
        if mode == "diff":
            lam = (jnp.exp(jnp.sum(lq1_ref[...] * lk1_ref[...], axis=-1, keepdims=True))
                   - jnp.exp(jnp.sum(lq2_ref[...] * lk2_ref[...], axis=-1, keepdims=True)) + lam_init)
            o = normalised(0) - lam * normalised(1)
            ms = jnp.mean(o * o, axis=-1, keepdims=True)
            o = o * lax.rsqrt(ms + RMS_EPS) * sg_ref[...] * (1.0 - lam_init)
            o_ref[...] = o.astype(o_ref.dtype)
        else:
            for r in range(n_rep):
                o_ref[:, r * dv:(r + 1) * dv] = normalised(r).astype(o_ref.dtype)


def _flash(mode, q, k, v, *, row0, batch, seq, n_groups, kr=None, diff_params=None,
           lam_init=0.0, k_col=None, v_col=None, tq=512, tk=1024):
    tq = _pick(seq, tq)
    tk = _pick(seq, tk)
    nq, nk = seq // tq, seq // tk
    rq, rk = row0 // tq, row0 // tk
    assert row0 % tq == 0 and row0 % tk == 0
    if mode == "diff":
        n_rep, qw, kw, vw, ow = 2, 2 * HEAD_DIM, 2 * HEAD_DIM, 2 * HEAD_DIM, 2 * HEAD_DIM
    elif mode == "gqa":
        n_rep = q.shape[1] // (n_groups * HEAD_DIM)
        qw, kw, vw, ow = n_rep * HEAD_DIM, HEAD_DIM, HEAD_DIM, n_rep * HEAD_DIM
    else:
        n_rep, qw, kw, vw, ow = 1, 2 * HEAD_DIM, HEAD_DIM, HEAD_DIM, HEAD_DIM
    k_col = k_col or (lambda h: h)
    v_col = v_col or (lambda h: h)

    in_specs = [
        pl.BlockSpec((tq, qw), lambda b, h, i, kk: (rq + b * nq + i, h)),
        pl.BlockSpec((tk, kw), lambda b, h, i, kk: (rk + b * nk + kk, k_col(h))),
    ]
    args = [q, k]
    if mode == "mla":
        in_specs.append(pl.BlockSpec((tk, HEAD_DIM), lambda b, h, i, kk: (rk + b * nk + kk, 0)))
        args.append(kr)
    in_specs.append(pl.BlockSpec((tk, vw), lambda b, h, i, kk: (rk + b * nk + kk, v_col(h))))
    args.append(v)
    if mode == "diff":
        for p in diff_params:
            in_specs.append(pl.BlockSpec((1, p.shape[1]), lambda b, h, i, kk: (0, 0)))
            args.append(p)
    dv = vw
    kern = functools.partial(_flash_kernel, mode=mode, n_rep=n_rep, nk=nk, lam_init=lam_init)
    return pl.pallas_call(
        kern,
        out_shape=jax.ShapeDtypeStruct((batch * seq, n_groups * ow), BF16),
        grid=(batch, n_groups, nq, nk),
        in_specs=in_specs,
        out_specs=pl.BlockSpec((tq, ow), lambda b, h, i, kk: (b * nq + i, h)),
        scratch_shapes=[pltpu.VMEM((n_rep, tq, LANES), F32), pltpu.VMEM((n_rep, tq, LANES), F32),
                        pltpu.VMEM((n_rep, tq, dv), F32),
                        pltpu.VMEM((n_rep, tq, tk), F32), pltpu.VMEM((n_rep, tq, tk), BF16)],
        compiler_params=_cparams("parallel", "parallel", "parallel", "arbitrary"),
        name="flash_" + mode,
    )(*args)


def _na_kernel(q_ref, k_ref, v_ref, bias_ref, o_ref, *, n_rows, heads, scale):
    r0 = pl.program_id(2) * NA_QROWS
    us = jnp.clip(r0 - NA_WIN_ROWS // 2, 0, n_rows - NA_UROWS)
    start = pl.multiple_of(us * GRID_W, GRID_W)
    kb = k_ref[pl.ds(start, NA_UROWS * GRID_W), :]
    vb = v_ref[pl.ds(start, NA_UROWS * GRID_W), :]
    for h in range(heads):
        sl = slice(h * HEAD_DIM, (h + 1) * HEAD_DIM)
        s = lax.dot_general(q_ref[:, sl], kb[:, sl], (((1,), (1,)), ((), ())),
                            preferred_element_type=F32)
        s = s * scale + bias_ref[h]
        m = jnp.max(s, axis=-1, keepdims=True)
        p = jnp.exp(s - m)
        l = jnp.sum(p, axis=-1, keepdims=True)
        o = jnp.dot(p.astype(BF16), vb[:, sl], preferred_element_type=F32) / l
        o_ref[:, sl] = o.astype(o_ref.dtype)


def _na_bias_table(rpb):
    c = np.arange(GRID_W)
    cs = np.clip(c - NA_WIN_COLS // 2, 0, GRID_W - NA_WIN_COLS)
    col_in = (c[None, :] >= cs[:, None]) & (c[None, :] < cs[:, None] + NA_WIN_COLS)
    dc_idx = np.clip(c[None, :] - c[:, None], -(NA_WIN_COLS - 1), NA_WIN_COLS - 1) + NA_WIN_COLS - 1
    t = np.arange(NA_QROWS)[:, None]
    j = np.arange(NA_UROWS)[None, :]
    half = NA_WIN_ROWS // 2
    own_start = np.stack([0 * t, t, 0 * t + (NA_UROWS - NA_WIN_ROWS)])
    q_row = np.stack([t, t + half, t + (NA_UROWS - NA_QROWS)])
    row_in = (j[None] >= own_start) & (j[None] < own_start + NA_WIN_ROWS)
    dr_idx = np.clip(j[None] - q_row + NA_WIN_ROWS - 1, 0, 2 * NA_WIN_ROWS - 2)
    rows = rpb.astype(F32)[:, dr_idx, :]
    onehot = (dc_idx[:, :, None] == np.arange(2 * NA_WIN_COLS - 1)).astype(np.float32)
    tab = jnp.einsum("hctjd,qkd->hctqjk", rows, onehot, precision=lax.Precision.HIGHEST)
    keep = row_in[None, :, :, None, :, None] & col_in[None, None, None, :, None, :]
    tab = jnp.where(keep, tab, MASK_VALUE)
    return tab.reshape(rpb.shape[0], 3, NA_QROWS * GRID_W, NA_UROWS * GRID_W)


def _neighbourhood(qkv, bias, *, row0, batch, seq, n_heads, heads_per_step=4):
    n_rows = seq // GRID_W
    assert n_rows >= NA_UROWS and n_rows % NA_QROWS == 0
    hb = heads_per_step
    nhb = n_heads // hb
    n_steps = n_rows // NA_QROWS
    qb = NA_QROWS * GRID_W
    r0 = row0 // qb
    b0 = row0 // seq
    assert row0 % seq == 0

    def bias_map(b, h, i):
        return (h, jnp.where(i == 0, 0, jnp.where(i == n_steps - 1, 2, 1)), 0, 0)

    kern = functools.partial(_na_kernel, n_rows=n_rows, heads=hb, scale=HEAD_DIM ** -0.5)
    return pl.pallas_call(
        kern,
        out_shape=jax.ShapeDtypeStruct((batch * seq, n_heads * HEAD_DIM), BF16),
        grid=(batch, nhb, n_steps),
        in_specs=[
            pl.BlockSpec((qb, hb * HEAD_DIM), lambda b, h, i: (r0 + b * n_steps + i, h)),
            pl.BlockSpec((seq, hb * HEAD_DIM), lambda b, h, i: (b0 + b, nhb + h)),
            pl.BlockSpec((seq, hb * HEAD_DIM), lambda b, h, i: (b0 + b, 2 * nhb + h)),
            pl.BlockSpec((hb, None, qb, NA_UROWS * GRID_W), bias_map),
        ],
        out_specs=pl.BlockSpec((qb, hb * HEAD_DIM), lambda b, h, i: (b * n_steps + i, h)),
        compiler_params=_cparams("parallel", "parallel", "arbitrary"),
        name="neighbourhood_attention",
    )(qkv, qkv, qkv, bias)


def _router_kernel(x_ref, w_ref, o_ref):
    logits = jnp.dot(x_ref[...], w_ref[...], preferred_element_type=F32,
                     precision=lax.Precision.HIGHEST)
    lane = lax.broadcasted_iota(jnp.int32, logits.shape, 1)
    logits = jnp.where(lane < N_EXPERTS, logits, -jnp.inf)
    m1 = jnp.max(logits, axis=-1, keepdims=True)
    i1 = jnp.min(jnp.where(logits == m1, lane, LANES), axis=-1, keepdims=True)
    rest = jnp.where(lane == i1, -jnp.inf, logits)
    m2 = jnp.max(rest, axis=-1, keepdims=True)
    i2 = jnp.min(jnp.where(rest == m2, lane, LANES), axis=-1, keepdims=True)
    e2 = jnp.exp(m2 - m1)
    g1 = 1.0 / (1.0 + e2)
    g2 = e2 / (1.0 + e2)
    out = jnp.where(lane == 0, i1.astype(F32), 0.0)
    out = jnp.where(lane == 1, i2.astype(F32), out)
    out = jnp.where(lane == 2, g1, out)
    out = jnp.where(lane == 3, g2, out)
    o_ref[...] = out


def _router(x, w_router, *, tm=512):
    m, d = x.shape
    tm = _pick(m, tm)
    w = jnp.zeros((d, LANES), F32).at[:, :N_EXPERTS].set(w_router.astype(F32))
    return pl.pallas_call(
        _router_kernel,
        out_shape=jax.ShapeDtypeStruct((m, LANES), F32),
        grid=(m // tm,),
        in_specs=[pl.BlockSpec((tm, d), lambda i: (i, 0)), pl.BlockSpec((d, LANES), lambda i: (0, 0))],
        out_specs=pl.BlockSpec((tm, LANES), lambda i: (i, 0)),
        compiler_params=_cparams("parallel"), name="router",
    )(x, w)


def _row_copy(src_hbm, row, dst, dst_row, sem):
    return pltpu.make_async_copy(src_hbm.at[pl.ds(row, 1)], dst.at[pl.ds(dst_row, 1)], sem)


def _gather_kernel(tok_ref, x_hbm, o_ref, buf, sem, *, rows):
    g = pl.program_id(0)
    slot = g % 2

    def fetch(step, into):
        def issue(t, c):
            _row_copy(x_hbm, tok_ref[step * rows + t], buf.at[into], t, sem.at[into]).start()
            return c
        lax.fori_loop(0, rows, issue, 0)

    @pl.when(g == 0)
    def _():
        fetch(0, 0)

    @pl.when(g + 1 < pl.num_programs(0))
    def _():
        fetch(g + 1, 1 - slot)

    def wait(t, c):
        _row_copy(x_hbm, 0, buf.at[slot], t, sem.at[slot]).wait()
        return c

    lax.fori_loop(0, rows, wait, 0)
    o_ref[...] = buf[slot].astype(o_ref.dtype)


def _gather_rows(x, slot_tok, *, rows=MOE_GROUP):
    n_slots = slot_tok.shape[0]
    d = x.shape[1]
    gs = pltpu.PrefetchScalarGridSpec(
        num_scalar_prefetch=1, grid=(n_slots // rows,),
        in_specs=[pl.BlockSpec(memory_space=pl.ANY)],
        out_specs=pl.BlockSpec((rows, d), lambda g, tok: (g, 0)),
        scratch_shapes=[pltpu.VMEM((2, rows, d), x.dtype), pltpu.SemaphoreType.DMA((2,))],
    )
    return pl.pallas_call(
        functools.partial(_gather_kernel, rows=rows),
        out_shape=jax.ShapeDtypeStruct((n_slots, d), BF16), grid_spec=gs,
        compiler_params=_cparams("arbitrary"), name="moe_gather",
    )(slot_tok, x)


def _combine_kernel(pos_ref, ys_hbm, x_ref, gate_ref, g_ref, b_ref, o32_ref, o16_ref, buf, sem, *, rows):
    i = pl.program_id(0)
    slot = i % 2

    def fetch(step, into):
        def issue(t, c):
            for kx in range(TOP_K):
                _row_copy(ys_hbm, pos_ref[(step * rows + t) * TOP_K + kx], buf.at[into, kx], t,
                          sem.at[into]).start()
            return c
        lax.fori_loop(0, rows, issue, 0)

    @pl.when(i == 0)
    def _():
        fetch(0, 0)

    @pl.when(i + 1 < pl.num_programs(0))
    def _():
        fetch(i + 1, 1 - slot)

    def wait(t, c):
        for kx in range(TOP_K):
            _row_copy(ys_hbm, 0, buf.at[slot, kx], t, sem.at[slot]).wait()
        return c

    lax.fori_loop(0, rows, wait, 0)
    gate = gate_ref[...]
    moe = buf[slot, 0] * gate[:, 0:1] + buf[slot, 1] * gate[:, 1:2]
    out = _layer_norm_rows(DEEPNORM_ALPHA * x_ref[...] + moe, g_ref[...], b_ref[...])
    o32_ref[...] = out
    o16_ref[...] = out.astype(BF16)


def _combine_ln(ys, pos, gates, x, g, b, *, rows=128):
    m, d = x.shape
    rows = _pick(m, rows)
    row = lambda i, pos: (i, 0)
    gs = pltpu.PrefetchScalarGridSpec(
        num_scalar_prefetch=1, grid=(m // rows,),
        in_specs=[pl.BlockSpec(memory_space=pl.ANY),
                  pl.BlockSpec((rows, d), row),
                  pl.BlockSpec((rows, TOP_K), row),
                  pl.BlockSpec((1, d), lambda i, pos: (0, 0)),
                  pl.BlockSpec((1, d), lambda i, pos: (0, 0))],
        out_specs=(pl.BlockSpec((rows, d), row), pl.BlockSpec((rows, d), row)),
        scratch_shapes=[pltpu.VMEM((2, TOP_K, rows, d), ys.dtype), pltpu.SemaphoreType.DMA((2,))],
    )
    return pl.pallas_call(
        functools.partial(_combine_kernel, rows=rows),
        out_shape=(jax.ShapeDtypeStruct((m, d), F32), jax.ShapeDtypeStruct((m, d), BF16)),
        grid_spec=gs, compiler_params=_cparams("arbitrary"), name="moe_combine_ln",
    )(pos.reshape(-1), ys, x, gates, g.reshape(1, d), b.reshape(1, d))


def _moe_layer(x32, x16, w_router, w_gate_up, w_down, ln_g, ln_b):
    n_tok, d = x32.shape
    fe = w_down.shape[1]
    route = _router(x32, w_router)
    e_idx = route[:, :TOP_K].astype(jnp.int32)
    gates = route[:, TOP_K:2 * TOP_K]
    flat_e = e_idx.reshape(-1)
    n_asg = flat_e.shape[0]
    onehot = (flat_e[:, None] == jnp.arange(N_EXPERTS, dtype=jnp.int32)[None, :]).astype(jnp.int32)
    csum = jnp.cumsum(onehot, axis=0)
    rank = jnp.sum((csum - onehot) * onehot, axis=1)
    counts = csum[-1]
    padded = (counts + MOE_GROUP - 1) // MOE_GROUP * MOE_GROUP
    padded_end = jnp.cumsum(padded)
    pos = (padded_end - padded)[flat_e] + rank
    n_slots = -(-n_asg // MOE_GROUP) * MOE_GROUP + N_EXPERTS * MOE_GROUP
    n_groups = n_slots // MOE_GROUP
    flat_tok = jnp.repeat(jnp.arange(n_tok, dtype=jnp.int32), TOP_K)
    slot_tok = jnp.zeros((n_slots,), jnp.int32).at[pos].set(flat_tok, unique_indices=True)
    group_start = jnp.arange(n_groups, dtype=jnp.int32) * MOE_GROUP
    group_expert = jnp.minimum(jnp.sum(padded_end[None, :] <= group_start[:, None], axis=1),
                               N_EXPERTS - 1).astype(jnp.int32)
    n_used = (padded_end[-1] // MOE_GROUP).astype(jnp.int32)
    group_expert = jnp.concatenate([group_expert, n_used[None]])

    xs = _gather_rows(x32, slot_tok)
    hs = _mm(xs, w_gate_up, n_out=fe, tm=MOE_GROUP, tn=512, swiglu=True, group_expert=group_expert)
    ys = _mm(hs, w_down, n_out=d, tm=MOE_GROUP, tn=512, out_dtype=F32, group_expert=group_expert)
    return _combine_ln(ys, pos.reshape(n_tok, TOP_K), gates, x32, ln_g, ln_b)


def _rope_tables(pos, dim):
    inv = ROPE_THETA ** (-jnp.arange(0, dim, 2, dtype=F32) / dim)
    ang = pos[:, None] * inv[None, :]
    ang = jnp.concatenate([ang, ang], axis=-1)
    sign = jnp.concatenate([-jnp.ones((dim // 2,), F32), jnp.ones((dim // 2,), F32)])
    return jnp.cos(ang), jnp.sin(ang) * sign[None, :]


def _lambda_init(layer_idx):
    return 0.8 - 0.6 * math.exp(-0.3 * layer_idx)


def kernel(x_prompt, x_sample, l0_w_qkv, l0_lambda_q1, l0_lambda_k1, l0_lambda_q2, l0_lambda_k2, l0_subln_g, l0_w_o, l0_ln1_g, l0_ln1_b, l0_ffn_w_gate_up, l0_ffn_w_down, l0_ln2_g, l0_ln2_b, l1_w_dqkv, l1_q_norm_g, l1_w_uq, l1_kv_norm_g, l1_w_ukv, l1_w_o, l1_ln1_g, l1_ln1_b, l1_router, l1_exp_w_gate_up, l1_exp_w_down, l1_ln2_g, l1_ln2_b, l2_w_qkv, l2_rpb, l2_w_o, l2_ln1_g, l2_ln1_b, l2_ffn_w_gate_up, l2_ffn_w_down, l2_ln2_g, l2_ln2_b, l3_w_qkv, l3_q_norm_g, l3_k_norm_g, l3_w_o, l3_ln1_g, l3_ln1_b, l3_router, l3_exp_w_gate_up, l3_exp_w_down, l3_ln2_g, l3_ln2_b):
    d = x_prompt.shape[-1]
    streams = ((x_prompt.shape[0], x_prompt.shape[1]), (x_sample.shape[0], x_sample.shape[1]))
    n_p = streams[0][0] * streams[0][1]
    n_s = streams[1][0] * streams[1][1]
    row0s = (0, n_p)
    n_heads = d // HEAD_DIM

    x32 = jnp.concatenate([x_prompt.reshape(n_p, d), x_sample.reshape(n_s, d)], axis=0)
    x16 = x32.astype(BF16)
    pos = jnp.concatenate([jnp.tile(jnp.arange(s, dtype=jnp.int32), b) for b, s in streams])
    posf = pos.astype(F32)

    def per_stream(fn):
        return jnp.concatenate([fn(row0, b, s) for row0, (b, s) in zip(row0s, streams)], axis=0)

    def dense_ffn(x32, x16, w_gate_up, w_down, g, b):
        f = w_down.shape[0]
        h = _mm(x16, w_gate_up.astype(BF16), n_out=f, tm=1024, tn=512, swiglu=True)
        y = _mm(h, w_down.astype(BF16), n_out=d, tm=256, tn=512, out_dtype=F32)
        return _residual_ln(x32, y, g, b)

    def attn_out(x32, o, w_o, g, b):
        y = _mm(o, w_o, n_out=d, out_dtype=F32)
        return _residual_ln(x32, y, g, b)

    w = l0_w_qkv
    rope128 = _rope_tables(posf, HEAD_DIM)
    q = _mm(x16, w, n_out=d, col0=0, rope=rope128, rope_w=HEAD_DIM, scale=HEAD_DIM ** -0.5 * LOG2E)
    k = _mm(x16, w, n_out=d, col0=d, rope=rope128, rope_w=HEAD_DIM)
    v = _mm(x16, w, n_out=d, col0=2 * d)
    lam0 = _lambda_init(0)
    diff_params = [p.astype(F32).reshape(1, -1) for p in
                   (l0_lambda_q1, l0_lambda_k1, l0_lambda_q2, l0_lambda_k2, l0_subln_g)]
    o = per_stream(lambda row0, b, s: _flash(
        "diff", q, k, v, row0=row0, batch=b, seq=s, n_groups=n_heads // 2,
        diff_params=diff_params, lam_init=lam0, tq=1024, tk=2048))
    x32, x16 = attn_out(x32, o, l0_w_o, l0_ln1_g, l0_ln1_b)
    x32, x16 = dense_ffn(x32, x16, l0_ffn_w_gate_up, l0_ffn_w_down, l0_ln2_g, l0_ln2_b)

    q_lora = l1_q_norm_g.shape[0]
    kv_lora = l1_kv_norm_g.shape[0]
    w = l1_w_dqkv.astype(BF16)
    c_q = _mm(x16, w, n_out=q_lora, col0=0, tn=q_lora,
              rms_gain=l1_q_norm_g.reshape(1, -1), rms_w=q_lora)
    c_kv = _mm(x16, w, n_out=kv_lora, col0=q_lora, tn=kv_lora,
               rms_gain=l1_kv_norm_g.reshape(1, -1), rms_w=kv_lora)
    cos64, sin64 = _rope_tables(posf, MLA_ROPE)
    n_tok = x32.shape[0]
    pad1 = jnp.ones((n_tok, HEAD_DIM - MLA_ROPE), F32)
    pad0 = jnp.zeros((n_tok, HEAD_DIM - MLA_ROPE), F32)
    kr_tab = (jnp.concatenate([cos64, pad1], axis=1), jnp.concatenate([sin64, pad0], axis=1))
    q_tab = (jnp.concatenate([jnp.ones((n_tok, MLA_NOPE), F32), kr_tab[0]], axis=1),
             jnp.concatenate([jnp.zeros((n_tok, MLA_NOPE), F32), kr_tab[1]], axis=1))
    w_kr = jnp.pad(w[:, q_lora + kv_lora:], ((0, 0), (0, HEAD_DIM - MLA_ROPE)))
    k_rope = _mm(x16, w_kr, n_out=HEAD_DIM, tn=HEAD_DIM, rope=kr_tab, rope_w=MLA_ROPE)
    w_uq = l1_w_uq.astype(BF16).reshape(q_lora, n_heads, MLA_NOPE + MLA_ROPE)
    w_uq = jnp.pad(w_uq, ((0, 0), (0, 0), (0, 2 * HEAD_DIM - MLA_NOPE - MLA_ROPE)))
    w_uq = w_uq.reshape(q_lora, n_heads * 2 * HEAD_DIM)
    q = _mm(c_q, w_uq, n_out=n_heads * 2 * HEAD_DIM, rope=q_tab, rope_w=MLA_ROPE,
            scale=(MLA_NOPE + MLA_ROPE) ** -0.5 * LOG2E)
    kv = _mm(c_kv, l1_w_ukv.astype(BF16), n_out=n_heads * (MLA_NOPE + MLA_V))
    o = per_stream(lambda row0, b, s: _flash(
        "mla", q, kv, kv, kr=k_rope, row0=row0, batch=b, seq=s, n_groups=n_heads,
        k_col=lambda h: 2 * h, v_col=lambda h: 2 * h + 1, tq=2048, tk=1024))
    x32, x16 = attn_out(x32, o, l1_w_o, l1_ln1_g, l1_ln1_b)
    x32, x16 = _moe_layer(x32, x16, l1_router, l1_exp_w_gate_up.astype(F32),
                          l1_exp_w_down.astype(BF16), l1_ln2_g, l1_ln2_b)

    qkv = _mm(x16, l2_w_qkv, n_out=3 * d)
    bias = _na_bias_table(l2_rpb)
    o = per_stream(lambda row0, b, s: _neighbourhood(
        qkv, bias, row0=row0, batch=b, seq=s, n_heads=n_heads))
    x32, x16 = attn_out(x32, o, l2_w_o, l2_ln1_g, l2_ln1_b)
    x32, x16 = dense_ffn(x32, x16, l2_ffn_w_gate_up, l2_ffn_w_down, l2_ln2_g, l2_ln2_b)

    w = l3_w_qkv
    kv_w = GQA_KV_HEADS * HEAD_DIM
    half = HEAD_DIM // 2
    cr, sr = _rope_tables((pos // GRID_W).astype(F32), half)
    cc, sc = _rope_tables((pos % GRID_W).astype(F32), half)
    axial = (jnp.concatenate([cr, cc], axis=1), jnp.concatenate([sr, sc], axis=1))
    qg = jnp.tile(l3_q_norm_g.astype(F32), n_heads).reshape(1, -1)
    kg = jnp.tile(l3_k_norm_g.astype(F32), GQA_KV_HEADS).reshape(1, -1)
    q = _mm(x16, w, n_out=d, col0=0, rms_gain=qg, rms_w=HEAD_DIM, rope=axial, rope_w=half,
            scale=HEAD_DIM ** -0.5 * LOG2E)
    k = _mm(x16, w, n_out=kv_w, col0=d, rms_gain=kg, rms_w=HEAD_DIM, rope=axial, rope_w=half)
    v = _mm(x16, w, n_out=kv_w, col0=d + kv_w)
    o = per_stream(lambda row0, b, s: _flash(
        "gqa", q, k, v, row0=row0, batch=b, seq=s, n_groups=GQA_KV_HEADS, tq=512, tk=2048))
    x32, x16 = attn_out(x32, o, l3_w_o, l3_ln1_g, l3_ln1_b)
    x32, x16 = _moe_layer(x32, x16, l3_router, l3_exp_w_gate_up.astype(F32),
                          l3_exp_w_down.astype(BF16), l3_ln2_g, l3_ln2_b)

    return (x32[:n_p].reshape(x_prompt.shape), x32[n_p:].reshape(x_sample.shape))
```
